```python
import jax, jax.numpy as jnp
from jax import lax
import numpy as np

D_MODEL = 1024
BATCH = 8
SEQ = 2048
DEPTH = 4
DEC_BATCH = 128
DEC_SEQ = 8
PAST_LEN = 8192
PAGE_SIZE = 128

N_EVEN = (DEPTH + 1) // 2
N_ODD = DEPTH // 2
D_A = D_MODEL // 2
CONV_A = 3
H_B = 8
DH_B = 64
H_C = 8
Q_LORA = 384
KV_LORA = 256
DH_NOPE = 64
DH_ROPE = 32
DH_V = 64
ROPE_BASE = 10000.0
D_D = D_MODEL // 2
CONV_D = 31
D_FF = 2816
CONV_FFN = 3
Q_BLOCK = 128
EPS = 1e-6

_HB = H_B * DH_B
EVEN_SPLITS = [D_A, 2 * D_A, 3 * D_A, 3 * D_A + _HB, 3 * D_A + 2 * _HB, 3 * D_A + 3 * _HB]
EVEN_IN = 3 * D_A + 3 * _HB + H_B
EVEN_OUT = D_A + _HB
ODD_SPLITS = [Q_LORA, Q_LORA + KV_LORA, Q_LORA + KV_LORA + DH_ROPE]
ODD_IN = Q_LORA + KV_LORA + DH_ROPE + 2 * D_D
ODD_OUT = H_C * DH_V + D_D

kernel_name = 'hybrid_conv_fox_mla_conformer_decode_step'

F32 = jnp.float32


def rmsnorm(x, g):
    xf = x.astype(F32)
    y = xf * lax.rsqrt(jnp.mean(xf * xf, axis=-1, keepdims=True) + EPS)
    return (y * g.astype(F32)).astype(x.dtype)


def layernorm(x, g, b):
    xf = x.astype(F32)
    mu = jnp.mean(xf, axis=-1, keepdims=True)
    var = jnp.mean(jnp.square(xf - mu), axis=-1, keepdims=True)
    return ((xf - mu) * lax.rsqrt(var + EPS) * g.astype(F32) + b.astype(F32)).astype(x.dtype)


def modulation(c, w, b):
    mod = jax.nn.silu(c) @ w + b
    shift, scale, gate = jnp.split(mod[:, None, :], 3, axis=-1)
    return shift, scale, gate


def causal_dwconv(x, buf, w):
    width, ch = w.shape
    xp = jnp.concatenate([buf.astype(x.dtype), x], axis=1)
    y = lax.conv_general_dilated(xp, w[:, None, :].astype(x.dtype), window_strides=(1,), padding='VALID',
                                 dimension_numbers=('NWC', 'WIO', 'NWC'), feature_group_count=ch)
    return y, xp[:, xp.shape[1] - (width - 1):]


def rope(x, pos):
    half = DH_ROPE // 2
    inv = ROPE_BASE ** (-jnp.arange(half, dtype=F32) / half)
    ang = pos.astype(F32)[:, None] * inv[None, :]
    cos, sin = jnp.cos(ang), jnp.sin(ang)
    if x.ndim == 4:
        cos, sin = cos[:, None, :], sin[:, None, :]
    x1, x2 = x[..., :half].astype(F32), x[..., half:].astype(F32)
    return jnp.concatenate([x1 * cos - x2 * sin, x1 * sin + x2 * cos], axis=-1).astype(x.dtype)


def causal_mask(q_start, n_q, n_k):
    return jnp.arange(n_k)[None, :] <= (q_start + jnp.arange(n_q))[:, None]


def masked_softmax(logits, mask):
    return jax.nn.softmax(jnp.where(mask, logits.astype(F32), -jnp.inf), axis=-1)


def sweep_query_blocks(block_fn, n_q):
    starts = jnp.arange(n_q // Q_BLOCK, dtype=jnp.int32) * Q_BLOCK
    out = jnp.moveaxis(lax.map(block_fn, starts), 0, 1)
    return out.reshape((out.shape[0], n_q) + out.shape[3:])


def fox_prompt(q, k, v, logf):
    n_t = q.shape[1]
    cum = jnp.cumsum(logf, axis=1).transpose(0, 2, 1)
    scale = DH_B ** -0.5

    def block(s):
        qb = lax.dynamic_slice_in_dim(q, s, Q_BLOCK, axis=1)
        cq = lax.dynamic_slice_in_dim(cum, s, Q_BLOCK, axis=2)
        logits = jnp.einsum('bqhd,bkhd->bhqk', qb, k).astype(F32) * scale + (cq[..., :, None] - cum[..., None, :])
        p = masked_softmax(logits, causal_mask(s, Q_BLOCK, n_t))
        return jnp.einsum('bhqk,bkhd->bqhd', p.astype(v.dtype), v)

    return sweep_query_blocks(block, n_t)


def fox_sample(q, k, v, logf, paged, j):
    db, n_t = q.shape[:2]
    pt = paged['page_table']
    n_past = pt.shape[1] * PAGE_SIZE
    kp = paged['fox_k'][j, pt].reshape(db, n_past, H_B, DH_B)
    vp = paged['fox_v'][j, pt].reshape(db, n_past, H_B, DH_B)
    lf = jnp.concatenate([paged['fox_logf'][j, pt].reshape(db, n_past, H_B).astype(F32), logf], axis=1)
    cum = jnp.cumsum(lf, axis=1).transpose(0, 2, 1)
    scale = DH_B ** -0.5
    logits = jnp.concatenate([jnp.einsum('bqhd,bkhd->bhqk', q, kp), jnp.einsum('bqhd,bkhd->bhqk', q, k)],
                             axis=-1).astype(F32) * scale
    logits = logits + (cum[..., n_past:][..., None] - cum[..., None, :])
    p = masked_softmax(logits, causal_mask(n_past, n_t, n_past + n_t)).astype(v.dtype)
    return (jnp.einsum('bhqk,bkhd->bqhd', p[..., :n_past], vp)
            + jnp.einsum('bhqk,bkhd->bqhd', p[..., n_past:], v))


def mla_prompt(q_lat, q_rope, latent, k_rope):
    n_t = q_lat.shape[1]
    scale = (DH_NOPE + DH_ROPE) ** -0.5

    def block(s):
        ql = lax.dynamic_slice_in_dim(q_lat, s, Q_BLOCK, axis=1)
        qr = lax.dynamic_slice_in_dim(q_rope, s, Q_BLOCK, axis=1)
        logits = (jnp.einsum('bqhc,bkc->bhqk', ql, latent)
                  + jnp.einsum('bqhr,bkr->bhqk', qr, k_rope)).astype(F32) * scale
        p = masked_softmax(logits, causal_mask(s, Q_BLOCK, n_t))
        return jnp.einsum('bhqk,bkc->bqhc', p.astype(latent.dtype), latent)

    return sweep_query_blocks(block, n_t)


def mla_sample(q_lat, q_rope, latent, k_rope, paged, j):
    db, n_t = q_lat.shape[:2]
    pt = paged['page_table']
    n_past = pt.shape[1] * PAGE_SIZE
    lat_p = paged['mla_latent'][j, pt].reshape(db, n_past, KV_LORA)
    kr_p = paged['mla_krope'][j, pt].reshape(db, n_past, DH_ROPE)
    scale = (DH_NOPE + DH_ROPE) ** -0.5
    l_past = jnp.einsum('bqhc,bkc->bhqk', q_lat, lat_p) + jnp.einsum('bqhr,bkr->bhqk', q_rope, kr_p)
    l_new = jnp.einsum('bqhc,bkc->bhqk', q_lat, latent) + jnp.einsum('bqhr,bkr->bhqk', q_rope, k_rope)
    logits = jnp.concatenate([l_past, l_new], axis=-1).astype(F32) * scale
    p = masked_softmax(logits, causal_mask(n_past, n_t, n_past + n_t)).astype(latent.dtype)
    return (jnp.einsum('bhqk,bkc->bqhc', p[..., :n_past], lat_p)
            + jnp.einsum('bhqk,bkc->bqhc', p[..., n_past:], latent))


def mixer_even(h, j, prm, buf, paged):
    b, n_t, _ = h.shape
    z = h @ prm['even_w_in'][j]
    xa, gb, gc, q, k, v, fl = jnp.split(z, EVEN_SPLITS, axis=-1)
    u, new_buf = causal_dwconv(gc * xa, buf, prm['conv_a_w'][j])
    ya = gb * u
    q = q.reshape(b, n_t, H_B, DH_B)
    k = k.reshape(b, n_t, H_B, DH_B)
    v = v.reshape(b, n_t, H_B, DH_B)
    logf = jax.nn.log_sigmoid(fl.astype(F32) + prm['fox_b_f'][j].astype(F32))
    if paged is None:
        yb = fox_prompt(q, k, v, logf)
    else:
        yb = fox_sample(q, k, v, logf, paged, j)
    y = jnp.concatenate([ya, yb.reshape(b, n_t, _HB)], axis=-1) @ prm['even_w_out'][j]
    return y, new_buf, k, v, logf


def mixer_odd(h, j, pos, prm, buf, paged):
    b, n_t, _ = h.shape
    z = h @ prm['odd_w_in'][j]
    cq, ckv, kr, glu = jnp.split(z, ODD_SPLITS, axis=-1)
    q = (rmsnorm(cq, prm['mla_q_norm'][j]) @ prm['mla_w_uq'][j]).reshape(b, n_t, H_C, DH_NOPE + DH_ROPE)
    q_nope = q[..., :DH_NOPE]
    q_rope = rope(q[..., DH_NOPE:], pos)
    latent = rmsnorm(ckv, prm['mla_kv_norm'][j])
    k_rope = rope(kr, pos)
    w_ukv = prm['mla_w_ukv'][j].reshape(KV_LORA, H_C, DH_NOPE + DH_V)
    q_lat = jnp.einsum('bthn,chn->bthc', q_nope, w_ukv[..., :DH_NOPE])
    if paged is None:
        o_lat = mla_prompt(q_lat, q_rope, latent, k_rope)
    else:
        o_lat = mla_sample(q_lat, q_rope, latent, k_rope, paged, j)
    yc = jnp.einsum('bthc,chv->bthv', o_lat, w_ukv[..., DH_NOPE:]).reshape(b, n_t, H_C * DH_V)
    a, g = jnp.split(glu, 2, axis=-1)
    u, new_buf = causal_dwconv(a * jax.nn.sigmoid(g), buf, prm['conv_d_w'][j])
    yd = jax.nn.silu(layernorm(u + prm['conv_d_b'][j], prm['conv_d_ln_g'][j], prm['conv_d_ln_b'][j]))
    y = jnp.concatenate([yc, yd], axis=-1) @ prm['odd_w_out'][j]
    return y, new_buf, latent, k_rope


def conv_ffn(h, i, prm, buf):
    u, g = jnp.split(h @ prm['ffn_w_up'][i], 2, axis=-1)
    u, new_buf = causal_dwconv(u, buf, prm['ffn_conv_w'][i])
    y = (jax.nn.silu(u + prm['ffn_conv_b'][i]) * g) @ prm['ffn_w_down'][i]
    return y, new_buf


def run_trunk(x, c, pos, prm, conv_a_in, conv_d_in, ffn_in, paged):
    b = x.shape[0]
    fk, fv, flf, lat, kro, ca, cd, cf = [], [], [], [], [], [], [], []
    for i in range(DEPTH):
        j = i // 2
        ng = prm['norm_g'][i]
        shift, scale, gate = modulation(c, prm['ada_w'][i, 0], prm['ada_b'][i, 0])
        h = rmsnorm(x, ng[0]) * (1 + scale) + shift
        if i % 2 == 0:
            buf = jnp.zeros((b, CONV_A - 1, D_A), x.dtype) if conv_a_in is None else conv_a_in[j]
            y, nb, k, v, lf = mixer_even(h, j, prm, buf, paged)
            ca.append(nb)
            fk.append(k)
            fv.append(v)
            flf.append(lf)
        else:
            buf = jnp.zeros((b, CONV_D - 1, D_D), x.dtype) if conv_d_in is None else conv_d_in[j]
            y, nb, lt, kr = mixer_odd(h, j, pos, prm, buf, paged)
            cd.append(nb)
            lat.append(lt)
            kro.append(kr)
        x = x + gate * rmsnorm(y, ng[1])
        shift, scale, gate = modulation(c, prm['ada_w'][i, 1], prm['ada_b'][i, 1])
        h = rmsnorm(x, ng[2]) * (1 + scale) + shift
        buf = jnp.zeros((b, CONV_FFN - 1, D_FF), x.dtype) if ffn_in is None else ffn_in[i]
        y, nb = conv_ffn(h, i, prm, buf)
        cf.append(nb)
        x = x + gate * rmsnorm(y, ng[3])
    return (x, jnp.stack(fk), jnp.stack(fv), jnp.stack(flf), jnp.stack(lat), jnp.stack(kro),
            jnp.stack(ca), jnp.stack(cd), jnp.stack(cf))


def setup_inputs(seed: int = 0) -> dict:
    key = jax.random.key(seed)
    ks = jax.random.split(key, 40)
    n_pages = PAST_LEN // PAGE_SIZE
    n_used = DEC_BATCH * n_pages
    n_pool = n_used + n_used // 4

    def nrm(i, shape, scale=1.0):
        return jax.random.normal(ks[i], shape, F32) * scale

    page_table = jax.random.permutation(ks[0], n_pool)[:n_used].reshape(DEC_BATCH, n_pages).astype(jnp.int32)
    return {
        'x_prompt': nrm(1, (BATCH, SEQ, D_MODEL)),
        'x_sample': nrm(2, (DEC_BATCH, DEC_SEQ, D_MODEL)),
        'cache_fox_k': nrm(3, (N_EVEN, n_pool, PAGE_SIZE, H_B, DH_B)),
        'cache_fox_v': nrm(4, (N_EVEN, n_pool, PAGE_SIZE, H_B, DH_B)),
        'cache_fox_logf': jax.nn.log_sigmoid(3.0 + nrm(5, (N_EVEN, n_pool, PAGE_SIZE, H_B))),
        'cache_mla_latent': nrm(6, (N_ODD, n_pool, PAGE_SIZE, KV_LORA)),
        'cache_mla_krope': nrm(7, (N_ODD, n_pool, PAGE_SIZE, DH_ROPE)),
        'state_conv_a': nrm(8, (N_EVEN, DEC_BATCH, CONV_A - 1, D_A)),
        'state_conv_d': nrm(9, (N_ODD, DEC_BATCH, CONV_D - 1, D_D)),
        'state_ffn_conv': nrm(10, (DEPTH, DEC_BATCH, CONV_FFN - 1, D_FF)),
        'page_table': page_table,
        'c_prompt': nrm(11, (BATCH, D_MODEL)),
        'c_sample': nrm(12, (DEC_BATCH, D_MODEL)),
        'ada_w': nrm(13, (DEPTH, 2, D_MODEL, 3 * D_MODEL), 0.5 * D_MODEL ** -0.5),
        'ada_b': nrm(14, (DEPTH, 2, 3 * D_MODEL), 0.01),
        'norm_g': 1.0 + nrm(15, (DEPTH, 4, D_MODEL), 0.05),
        'even_w_in': nrm(16, (N_EVEN, D_MODEL, EVEN_IN), D_MODEL ** -0.5),
        'conv_a_w': nrm(17, (N_EVEN, CONV_A, D_A), CONV_A ** -0.5),
        'fox_b_f': 3.0 + nrm(18, (N_EVEN, H_B), 0.5),
        'even_w_out': nrm(19, (N_EVEN, EVEN_OUT, D_MODEL), EVEN_OUT ** -0.5),
        'odd_w_in': nrm(20, (N_ODD, D_MODEL, ODD_IN), D_MODEL ** -0.5),
        'mla_q_norm': 1.0 + nrm(21, (N_ODD, Q_LORA), 0.05),
        'mla_w_uq': nrm(22, (N_ODD, Q_LORA, H_C * (DH_NOPE + DH_ROPE)), Q_LORA ** -0.5),
        'mla_kv_norm': 1.0 + nrm(23, (N_ODD, KV_LORA), 0.05),
        'mla_w_ukv': nrm(24, (N_ODD, KV_LORA, H_C * (DH_NOPE + DH_V)), KV_LORA ** -0.5),
        'conv_d_w': nrm(25, (N_ODD, CONV_D, D_D), CONV_D ** -0.5),
        'conv_d_b': nrm(26, (N_ODD, D_D), 0.02),
        'conv_d_ln_g': 1.0 + nrm(27, (N_ODD, D_D), 0.05),
        'conv_d_ln_b': nrm(28, (N_ODD, D_D), 0.02),
        'odd_w_out': nrm(29, (N_ODD, ODD_OUT, D_MODEL), ODD_OUT ** -0.5),
        'ffn_w_up': nrm(30, (DEPTH, D_MODEL, 2 * D_FF), D_MODEL ** -0.5),
        'ffn_conv_w': nrm(31, (DEPTH, CONV_FFN, D_FF), CONV_FFN ** -0.5),
        'ffn_conv_b': nrm(32, (DEPTH, D_FF), 0.02),
        'ffn_w_down': nrm(33, (DEPTH, D_FF, D_MODEL), D_FF ** -0.5),
    }


def reference(x_prompt, x_sample, cache_fox_k, cache_fox_v, cache_fox_logf, cache_mla_latent, cache_mla_krope,
              state_conv_a, state_conv_d, state_ffn_conv, page_table, c_prompt, c_sample,
              ada_w, ada_b, norm_g, even_w_in, conv_a_w, fox_b_f, even_w_out,
              odd_w_in, mla_q_norm, mla_w_uq, mla_kv_norm, mla_w_ukv, conv_d_w, conv_d_b, conv_d_ln_g, conv_d_ln_b,
              odd_w_out, ffn_w_up, ffn_conv_w, ffn_conv_b, ffn_w_down):
    prm = dict(ada_w=ada_w, ada_b=ada_b, norm_g=norm_g, even_w_in=even_w_in, conv_a_w=conv_a_w, fox_b_f=fox_b_f,
               even_w_out=even_w_out, odd_w_in=odd_w_in, mla_q_norm=mla_q_norm, mla_w_uq=mla_w_uq,
               mla_kv_norm=mla_kv_norm, mla_w_ukv=mla_w_ukv, conv_d_w=conv_d_w, conv_d_b=conv_d_b,
               conv_d_ln_g=conv_d_ln_g, conv_d_ln_b=conv_d_ln_b, odd_w_out=odd_w_out, ffn_w_up=ffn_w_up,
               ffn_conv_w=ffn_conv_w, ffn_conv_b=ffn_conv_b, ffn_w_down=ffn_w_down)
    paged = dict(page_table=page_table, fox_k=cache_fox_k, fox_v=cache_fox_v, fox_logf=cache_fox_logf,
                 mla_latent=cache_mla_latent, mla_krope=cache_mla_krope)
    pos_prompt = jnp.arange(x_prompt.shape[1], dtype=jnp.int32)
    past_len = page_table.shape[1] * PAGE_SIZE
    pos_sample = past_len + jnp.arange(x_sample.shape[1], dtype=jnp.int32)
    (y_prompt, p_fox_k, p_fox_v, p_fox_logf, p_mla_latent, p_mla_krope,
     p_conv_a, p_conv_d, p_ffn_conv) = run_trunk(x_prompt, c_prompt, pos_prompt, prm, None, None, None, None)
    (y_sample, s_fox_k, s_fox_v, s_fox_logf, s_mla_latent, s_mla_krope,
     s_conv_a, s_conv_d, s_ffn_conv) = run_trunk(x_sample, c_sample, pos_sample, prm,
                                                state_conv_a, state_conv_d, state_ffn_conv, paged)
    return (y_prompt, y_sample,
            p_fox_k, p_fox_v, p_fox_logf, p_mla_latent, p_mla_krope, p_conv_a, p_conv_d, p_ffn_conv,
            s_fox_k, s_fox_v, s_fox_logf, s_mla_latent, s_mla_krope, s_conv_a, s_conv_d, s_ffn_conv)
```

```python
import functools
from typing import NamedTuple

import jax
import jax.numpy as jnp
from jax import lax
from jax.experimental import pallas as pl
from jax.experimental.pallas import tpu as pltpu

F32 = jnp.float32
BF16 = jnp.bfloat16

EPS = 1e-6
PAGE_ROWS = 128
N_HEADS = 8
DH_FOX = 64
DH_NOPE = 64
DH_ROPE = 32
DH_V = 64
ROPE_BASE = 10000.0
CONV_A_HALO = 2
CONV_D_HALO = 30
CONV_FFN_HALO = 2
LANES = 128
SUBLANES = 8
NEG_BIG = -1e30
VMEM_LIMIT = 56 * 1024 * 1024
PROMPT_TILE = 512
SAMPLE_TILE = 512
FOX_TQ = 256
MLA_TQ = 128
MLA_TK = 256
PAGES_PER_STEP = 16
CONV_CHUNK_ROWS = 64


class _Tiling(NamedTuple):
    groups: int
    rows: int
    tiles_per_seq: int
    n_tiles: int


def _round_up(x, m):
    return (x + m - 1) // m * m


def _dot(a, b):
    return jnp.dot(a, b, preferred_element_type=F32)


def _dot_nt(a, b):
    return lax.dot_general(a, b, (((1,), (1,)), ((), ())), preferred_element_type=F32)


def _rms(x, g):
    return x * lax.rsqrt(jnp.mean(x * x, axis=-1, keepdims=True) + EPS) * g


def _params(n_axes=1):
    return pltpu.CompilerParams(dimension_semantics=("arbitrary",) * n_axes,
                                vmem_limit_bytes=VMEM_LIMIT)


def _const_spec(arr):
    nd = arr.ndim
    return pl.BlockSpec(arr.shape, lambda *_: (0,) * nd, pipeline_mode=pl.Buffered(1))


def _mod_spec(til, m, comp, d):
    return pl.BlockSpec((None, til.groups, 1, d), lambda i: (m, i // til.tiles_per_seq, 0, comp))


def _x_spec(til, c):
    return pl.BlockSpec((til.groups, til.rows, c), lambda i: (i, 0, 0))


def _row_spec(til, c):
    return pl.BlockSpec((til.groups * til.rows, c), lambda i: (i, 0))


def _state_spec(til, halo, c):
    return pl.BlockSpec((til.groups, halo, c), lambda i: (i // til.tiles_per_seq, 0, 0))


def _head_spec(til, c):
    return pl.BlockSpec((til.groups, N_HEADS, til.rows, c),
                        lambda i: (i // til.tiles_per_seq, 0, i % til.tiles_per_seq, 0))


def _tab_spec(til):
    return pl.BlockSpec((til.rows, LANES), lambda i: (i % til.tiles_per_seq, 0))


def _stage_conv_input(xp_ref, new, st_ref, halo, til):
    rows = til.rows
    hp = _round_up(halo, SUBLANES)
    lo = hp - halo
    if til.tiles_per_seq > 1:
        first = (pl.program_id(0) % til.tiles_per_seq) == 0

        @pl.when(first)
        def _():
            xp_ref[:, lo:hp, :] = st_ref[...]

        @pl.when(jnp.logical_not(first))
        def _():
            xp_ref[:, lo:hp, :] = xp_ref[:, lo + rows:hp + rows, :]
    else:
        xp_ref[:, lo:hp, :] = st_ref[...]
    xp_ref[:, hp:hp + rows, :] = new
    return lo


def _conv_taps(xp_ref, w_ref, lo, halo, gsl, r0, nrows):
    acc = w_ref[0:1, :] * xp_ref[gsl, lo + r0:lo + r0 + nrows, :]
    for k in range(1, halo + 1):
        acc = acc + w_ref[k:k + 1, :] * xp_ref[gsl, lo + r0 + k:lo + r0 + k + nrows, :]
    return acc


def _mod_body(cp_ref, cs_ref, w_ref, b_ref, op_ref, os_ref):
    w = w_ref[...].astype(BF16)
    b = b_ref[...]
    for c_ref, o_ref in ((cp_ref, op_ref), (cs_ref, os_ref)):
        c = c_ref[...]
        a = (c * jax.nn.sigmoid(c)).astype(BF16)
        o_ref[...] = _dot(a, w) + b


def _modulation(c_p, c_s, ada_w, ada_b):
    d = c_p.shape[1]
    n_mod = ada_w.shape[0] * ada_w.shape[1]
    w = ada_w.reshape(n_mod, d, 3 * d)
    b = ada_b.reshape(n_mod, 1, 3 * d)
    bp, bs = c_p.shape[0], c_s.shape[0]
    out_p, out_s = pl.pallas_call(
        _mod_body,
        grid=(n_mod, 3),
        in_specs=[pl.BlockSpec((bp, d), lambda m, n: (0, 0)),
                  pl.BlockSpec((bs, d), lambda m, n: (0, 0)),
                  pl.BlockSpec((None, d, d), lambda m, n: (m, 0, n)),
                  pl.BlockSpec((None, 1, d), lambda m, n: (m, 0, n))],
        out_specs=[pl.BlockSpec((None, bp, d), lambda m, n: (m, 0, n)),
                   pl.BlockSpec((None, bs, d), lambda m, n: (m, 0, n))],
        out_shape=[jax.ShapeDtypeStruct((n_mod, bp, 3 * d), F32),
                   jax.ShapeDtypeStruct((n_mod, bs, 3 * d), F32)],
        compiler_params=_params(2),
        name="adaln_modulation",
    )(c_p, c_s, w, b)
    return out_p.reshape(n_mod, bp, 1, 3 * d), out_s.reshape(n_mod, bs, 1, 3 * d)


def _even_in_body(til, want_lft, x_ref, sh_ref, sc_ref, ng_ref, w_ref, wfl_ref, bf_ref, cw_ref, st_ref,
                  ya_ref, q_ref, k_ref, v_ref, lf_ref, *rest):
    if want_lft:
        lft_ref, so_ref, xp_ref = rest
    else:
        so_ref, xp_ref = rest
    g, r = til.groups, til.rows
    tm = g * r
    x = x_ref[...]
    d = x.shape[-1]
    h = _rms(x, ng_ref[...]) * (1.0 + sc_ref[...]) + sh_ref[...]
    h2 = h.reshape(tm, d).astype(BF16)
    da = ya_ref.shape[-1]

    def proj(c):
        return _dot(h2, w_ref[:, c * da:(c + 1) * da])

    xa, gb, gc = proj(0), proj(1), proj(2)
    q_ref[...] = proj(3)
    k_ref[...] = proj(4)
    v_ref[...] = proj(5)
    lf = jax.nn.log_sigmoid(_dot(h2, wfl_ref[...]) + bf_ref[...])
    lf_ref[...] = lf[:, :N_HEADS]
    if want_lft:
        lft_ref[...] = lf.T[:N_HEADS, :]
    p = (gc * xa).reshape(g, r, da)
    lo = _stage_conv_input(xp_ref, p, st_ref, CONV_A_HALO, til)
    u = _conv_taps(xp_ref, cw_ref, lo, CONV_A_HALO, slice(None), 0, r)
    ya_ref[...] = gb * u.reshape(tm, da)
    so_ref[...] = xp_ref[:, lo + r:lo + r + CONV_A_HALO, :]


def _even_in(til, n_seq, seq_len, x, mod, m, ng, w_main, w_fl, b_f, conv_w, state, want_lft):
    n, d = x.shape
    da = conv_w.shape[1]
    tm = til.groups * til.rows
    x3 = x.reshape(n // til.rows, til.rows, d)
    out_shape = [jax.ShapeDtypeStruct((n, da), F32)] * 4 + [jax.ShapeDtypeStruct((n, N_HEADS), F32)]
    out_specs = [_row_spec(til, da)] * 4 + [_row_spec(til, N_HEADS)]
    if want_lft:
        out_shape.append(jax.ShapeDtypeStruct((n_seq, N_HEADS, seq_len), F32))
        out_specs.append(pl.BlockSpec((None, N_HEADS, til.rows),
                                      lambda i: (i // til.tiles_per_seq, 0, i % til.tiles_per_seq)))
    out_shape.append(jax.ShapeDtypeStruct((n_seq, CONV_A_HALO, da), F32))
    out_specs.append(_state_spec(til, CONV_A_HALO, da))
    return pl.pallas_call(
        functools.partial(_even_in_body, til, want_lft),
        grid=(til.n_tiles,),
        in_specs=[_x_spec(til, d), _mod_spec(til, m, 0, d), _mod_spec(til, m, 1, d), _const_spec(ng),
                  _const_spec(w_main), _const_spec(w_fl), _const_spec(b_f), _const_spec(conv_w),
                  _state_spec(til, CONV_A_HALO, da)],
        out_specs=out_specs,
        out_shape=out_shape,
        scratch_shapes=[pltpu.VMEM((til.groups, SUBLANES + til.rows, da), F32)],
        compiler_params=_params(1),
        name="even_in",
    )(x3, mod, mod, ng, w_main, w_fl, b_f, conv_w, state)


def _out_res_body(til, a1_ref, a2_ref, w1_ref, w2_ref, x_ref, gate_ref, ng_ref, xo_ref):
    g, r = til.groups, til.rows
    y = _dot(a1_ref[...].astype(BF16), w1_ref[...]) + _dot(a2_ref[...].astype(BF16), w2_ref[...])
    y3 = y.reshape(g, r, y.shape[-1])
    xo_ref[...] = x_ref[...] + gate_ref[...] * _rms(y3, ng_ref[...])


def _out_res(til, a1, a2, w1, w2, x, mod, m, ng):
    n, d = x.shape
    x3 = x.reshape(n // til.rows, til.rows, d)
    out = pl.pallas_call(
        functools.partial(_out_res_body, til),
        grid=(til.n_tiles,),
        in_specs=[_row_spec(til, a1.shape[1]), _row_spec(til, a2.shape[1]), _const_spec(w1), _const_spec(w2),
                  _x_spec(til, d), _mod_spec(til, m, 2, d), _const_spec(ng)],
        out_specs=_x_spec(til, d),
        out_shape=jax.ShapeDtypeStruct(x3.shape, F32),
        compiler_params=_params(1),
        name="even_out",
    )(a1, a2, w1, w2, x3, mod, ng)
    return out.reshape(n, d)


def _cumsum_lanes(x):
    n = x.shape[-1]
    idx = lax.broadcasted_iota(jnp.int32, x.shape, x.ndim - 1)
    s = 1
    while s < n:
        x = x + jnp.where(idx >= s, pltpu.roll(x, s, axis=x.ndim - 1), 0.0)
        s *= 2
    return x


def _softmax_step(carry, s, pv):
    m, l, acc = carry
    m_new = jnp.maximum(m, jnp.max(s, axis=1, keepdims=True))
    alpha = jnp.exp(m - m_new)
    p = jnp.exp(s - m_new)
    l = alpha * l + jnp.sum(p, axis=1, keepdims=True)
    acc = alpha * acc + pv(p.astype(BF16))
    return m_new, l, acc


def _fox_prompt_body(tq, nq, q_ref, k_ref, v_ref, lft_ref, o_ref, cum_ref):
    qi = pl.program_id(1)

    @pl.when(qi == 0)
    def _():
        c = _cumsum_lanes(lft_ref[...])
        for j in range(nq):
            cum_ref[j] = c[:, j * tq:(j + 1) * tq]

    scale = DH_FOX ** -0.5
    lane = lax.broadcasted_iota(jnp.int32, (tq, LANES), 1)
    row = lax.broadcasted_iota(jnp.int32, (tq, tq), 0)
    col = lax.broadcasted_iota(jnp.int32, (tq, tq), 1)
    for cg in range(N_HEADS * DH_FOX // LANES):
        csl = slice(cg * LANES, (cg + 1) * LANES)
        qc = q_ref[:, csl] * scale
        outs = []
        for e in range(LANES // DH_FOX):
            h = cg * (LANES // DH_FOX) + e
            qm = jnp.where(lane // DH_FOX == e, qc, 0.0).astype(BF16)

            def step(j, carry, masked, qm=qm, h=h, csl=csl):
                start = pl.multiple_of(j * tq, tq)
                kb = k_ref[pl.ds(start, tq), csl].astype(BF16)
                vb = v_ref[pl.ds(start, tq), csl].astype(BF16)
                s = _dot_nt(qm, kb) - cum_ref[j][h:h + 1, :]
                if masked:
                    s = jnp.where(col <= row, s, NEG_BIG)
                return _softmax_step(carry, s, lambda p: _dot(p, vb))

            init = (jnp.full((tq, 1), NEG_BIG, F32), jnp.zeros((tq, 1), F32), jnp.zeros((tq, LANES), F32))
            carry = lax.fori_loop(0, qi, lambda j, c: step(j, c, False), init)
            _, l, acc = step(qi, carry, True)
            outs.append(acc / l)
        o_ref[:, csl] = jnp.where(lane < DH_FOX, outs[0], outs[1])


def _fox_prompt(n_seq, seq_len, q, k, v, lft):
    n, c = q.shape
    tq = FOX_TQ
    nq = seq_len // tq
    return pl.pallas_call(
        functools.partial(_fox_prompt_body, tq, nq),
        grid=(n_seq, nq),
        in_specs=[pl.BlockSpec((tq, c), lambda b, i: (b * nq + i, 0)),
                  pl.BlockSpec((seq_len, c), lambda b, i: (b, 0)),
                  pl.BlockSpec((seq_len, c), lambda b, i: (b, 0)),
                  pl.BlockSpec((None, N_HEADS, seq_len), lambda b, i: (b, 0, 0))],
        out_specs=pl.BlockSpec((tq, c), lambda b, i: (b * nq + i, 0)),
        out_shape=jax.ShapeDtypeStruct((n, c), F32),
        scratch_shapes=[pltpu.VMEM((nq, N_HEADS, tq), F32)],
        compiler_params=_params(2),
        name="fox_prompt_attention",
    )(q, k, v, lft)


def _rows_to_lanes(x, pad_ref):
    pad_ref[...] = jnp.zeros(pad_ref.shape, F32)
    pad_ref[0:x.shape[0], 0:x.shape[1]] = x
    return pad_ref[...].T[0:x.shape[1], :]


def _pad_rows(x, rows):
    return jnp.concatenate([x, jnp.zeros((rows - x.shape[0], x.shape[1]), x.dtype)], axis=0)


def _fox_sample_body(pps, nc, dt, pt_ref, q_ref, kn_ref, vn_ref, lfn_ref, *rest):
    k_refs = rest[:pps]
    v_refs = rest[pps:2 * pps]
    lf_refs = rest[2 * pps:3 * pps]
    o_ref, m_ref, l_ref, acc_ref, pre_ref, pad_ref = rest[3 * pps:]
    c = pl.program_id(1)
    nrow = N_HEADS * dt
    width = N_HEADS * DH_FOX

    @pl.when(c == 0)
    def _():
        m_ref[...] = jnp.full(m_ref.shape, NEG_BIG, F32)
        l_ref[...] = jnp.zeros(l_ref.shape, F32)
        acc_ref[...] = jnp.zeros(acc_ref.shape, F32)
        pre_ref[...] = jnp.zeros(pre_ref.shape, F32)

    q = q_ref[...] * (DH_FOX ** -0.5)
    qt = jnp.broadcast_to(q[None], (N_HEADS, dt, width)).reshape(nrow, width)
    rh = lax.broadcasted_iota(jnp.int32, (nrow, width), 0) // dt
    ch = lax.broadcasted_iota(jnp.int32, (nrow, width), 1) // DH_FOX
    qbd = jnp.where(rh == ch, qt, 0.0).astype(BF16)

    def expand_heads(x):
        return jnp.broadcast_to(x[:, None, :], (N_HEADS, dt, x.shape[-1])).reshape(nrow, x.shape[-1])

    def update(s, pv):
        carry = _softmax_step((m_ref[...], l_ref[...], acc_ref[...]), s, pv)
        m_ref[...], l_ref[...], acc_ref[...] = carry

    lft = jnp.concatenate([_rows_to_lanes(r[...], pad_ref) for r in lf_refs], axis=1)
    cum = _cumsum_lanes(lft) + pre_ref[...]
    pre_ref[...] = cum[:, cum.shape[1] - 1:]
    s = jnp.concatenate([_dot_nt(qbd, r[...].astype(BF16)) for r in k_refs], axis=1) - expand_heads(cum)

    def pv_past(p):
        out = _dot(p[:, 0:PAGE_ROWS], v_refs[0][...].astype(BF16))
        for i in range(1, pps):
            out = out + _dot(p[:, i * PAGE_ROWS:(i + 1) * PAGE_ROWS], v_refs[i][...].astype(BF16))
        return out

    update(s, pv_past)

    @pl.when(c == nc - 1)
    def _():
        lfn = _rows_to_lanes(lfn_ref[...], pad_ref)
        cumn = _cumsum_lanes(lfn) + pre_ref[...]
        kn = _pad_rows(kn_ref[...], PAGE_ROWS).astype(BF16)
        vn = _pad_rows(vn_ref[...], PAGE_ROWS).astype(BF16)
        sn = _dot_nt(qbd, kn) - expand_heads(cumn)
        t = lax.broadcasted_iota(jnp.int32, (nrow, PAGE_ROWS), 0) % dt
        key = lax.broadcasted_iota(jnp.int32, (nrow, PAGE_ROWS), 1)
        sn = jnp.where(key <= t, sn, NEG_BIG)
        update(sn, lambda p: _dot(p, vn))
        o = (acc_ref[...] / l_ref[...]).reshape(N_HEADS, dt, width)
        hh = lax.broadcasted_iota(jnp.int32, (N_HEADS, dt, width), 0)
        cc = lax.broadcasted_iota(jnp.int32, (N_HEADS, dt, width), 2) // DH_FOX
        o_ref[...] = jnp.sum(jnp.where(hh == cc, o, 0.0), axis=0)


def _page_specs(n, layer, n_pages, pps, rows, c):
    return [pl.BlockSpec((None, None, rows, c),
                         lambda b, ch, pt, i=i: (layer, pt[b * n_pages + ch * pps + i], 0, 0))
            for i in range(n)]


def _fox_sample(layer, page_table, q, k_new, v_new, lf_new, cache_k, cache_v, cache_lf):
    db, n_pages = page_table.shape
    n, width = q.shape
    dt = n // db
    pps = PAGES_PER_STEP
    nc = n_pages // pps
    n_pool = cache_k.shape[1]
    ck = cache_k.reshape(cache_k.shape[0], n_pool, PAGE_ROWS, width)
    cv = cache_v.reshape(cache_v.shape[0], n_pool, PAGE_ROWS, width)
    nrow = N_HEADS * dt
    tok = lambda cdim: pl.BlockSpec((dt, cdim), lambda b, ch, pt: (b, 0))
    grid_spec = pltpu.PrefetchScalarGridSpec(
        num_scalar_prefetch=1,
        grid=(db, nc),
        in_specs=[tok(width), tok(width), tok(width), tok(N_HEADS)]
        + _page_specs(pps, layer, n_pages, pps, PAGE_ROWS, width)
        + _page_specs(pps, layer, n_pages, pps, PAGE_ROWS, width)
        + _page_specs(pps, layer, n_pages, pps, PAGE_ROWS, N_HEADS),
        out_specs=tok(width),
        scratch_shapes=[pltpu.VMEM((nrow, 1), F32), pltpu.VMEM((nrow, 1), F32), pltpu.VMEM((nrow, width), F32),
                        pltpu.VMEM((N_HEADS, 1), F32), pltpu.VMEM((PAGE_ROWS, LANES), F32)],
    )
    return pl.pallas_call(
        functools.partial(_fox_sample_body, pps, nc, dt),
        grid_spec=grid_spec,
        out_shape=jax.ShapeDtypeStruct((n, width), F32),
        compiler_params=_params(2),
        name="fox_sample_attention",
    )(page_table.reshape(-1), q, k_new, v_new, lf_new, *([ck] * pps), *([cv] * pps), *([cache_lf] * pps))


def _odd_in_body(til, x_ref, sh_ref, sc_ref, ng_ref, w_ref, wkr_ref, qn_ref, kvn_ref, wuq_ref, wuk_ref, sel_ref,
                 cos8_ref, sin8_ref, cosk_ref, sink_ref, ql_ref, qr_ref, lat_ref, kro_ref, ag_ref):
    g, r = til.groups, til.rows
    tm = g * r
    x = x_ref[...]
    d = x.shape[-1]
    h = _rms(x, ng_ref[...]) * (1.0 + sc_ref[...]) + sh_ref[...]
    h2 = h.reshape(tm, d).astype(BF16)
    q_lora = qn_ref.shape[-1]
    kv_lora = kvn_ref.shape[-1]
    d_d = ag_ref.shape[-1]
    o_kv = q_lora
    o_a = o_kv + kv_lora
    o_g = o_a + d_d

    def bcast_tab(ref):
        return jnp.broadcast_to(ref[...][None], (g, r, LANES)).reshape(tm, LANES)

    cq = _rms(_dot(h2, w_ref[:, 0:o_kv]), qn_ref[...]).astype(BF16)
    qq = _dot(cq, wuq_ref[...])
    n_nope = N_HEADS * DH_NOPE
    r1 = qq[:, n_nope:n_nope + LANES]
    r2 = qq[:, n_nope + LANES:n_nope + 2 * LANES]
    cos8, sin8 = bcast_tab(cos8_ref), bcast_tab(sin8_ref)
    qr_all = jnp.concatenate([r1 * cos8 - r2 * sin8, r1 * sin8 + r2 * cos8], axis=1).astype(BF16)
    lane = lax.broadcasted_iota(jnp.int32, (tm, LANES), 1)
    per = LANES // DH_NOPE
    for hd in range(N_HEADS):
        cg, e = hd // per, hd % per
        qn = jnp.where(lane // DH_NOPE == e, qq[:, cg * LANES:(cg + 1) * LANES], 0.0).astype(BF16)
        ql_ref[:, hd] = _dot(qn, wuk_ref[cg]).reshape(g, r, kv_lora)
        qr_ref[:, hd] = _dot(qr_all, sel_ref[hd]).reshape(g, r, DH_ROPE)

    lat_ref[...] = _rms(_dot(h2, w_ref[:, o_kv:o_a]), kvn_ref[...])
    kr = _dot(h2, wkr_ref[...])
    half = DH_ROPE // 2
    swapped = jnp.where(lane < half, -pltpu.roll(kr, LANES - half, axis=1), pltpu.roll(kr, half, axis=1))
    kro = kr * bcast_tab(cosk_ref) + swapped * bcast_tab(sink_ref)
    kro_ref[...] = kro[:, 0:DH_ROPE]

    a = _dot(h2, w_ref[:, o_a:o_g])
    gt = _dot(h2, w_ref[:, o_g:o_g + d_d])
    ag_ref[...] = a * jax.nn.sigmoid(gt)


def _odd_in(til, n_seq, seq_len, x, mod, m, ng, w_main, w_kr, qn, kvn, wuq, wuk, sel, tabs):
    n, d = x.shape
    x3 = x.reshape(n // til.rows, til.rows, d)
    kv_lora = kvn.shape[-1]
    d_d = (w_main.shape[1] - qn.shape[-1] - kv_lora) // 2
    cos8, sin8, cosk, sink = tabs
    return pl.pallas_call(
        functools.partial(_odd_in_body, til),
        grid=(til.n_tiles,),
        in_specs=[_x_spec(til, d), _mod_spec(til, m, 0, d), _mod_spec(til, m, 1, d), _const_spec(ng),
                  _const_spec(w_main), _const_spec(w_kr), _const_spec(qn), _const_spec(kvn), _const_spec(wuq),
                  _const_spec(wuk), _const_spec(sel),
                  _tab_spec(til), _tab_spec(til), _tab_spec(til), _tab_spec(til)],
        out_specs=[_head_spec(til, kv_lora), _head_spec(til, DH_ROPE), _row_spec(til, kv_lora),
                   _row_spec(til, DH_ROPE), _row_spec(til, d_d)],
        out_shape=[jax.ShapeDtypeStruct((n_seq, N_HEADS, seq_len, kv_lora), F32),
                   jax.ShapeDtypeStruct((n_seq, N_HEADS, seq_len, DH_ROPE), F32),
                   jax.ShapeDtypeStruct((n, kv_lora), F32),
                   jax.ShapeDtypeStruct((n, DH_ROPE), F32),
                   jax.ShapeDtypeStruct((n, d_d), F32)],
        compiler_params=_params(1),
        name="odd_in",
    )(x3, mod, mod, ng, w_main, w_kr, qn, kvn, wuq, wuk, sel, cos8, sin8, cosk, sink)


def _mla_prompt_body(tq, tk, ql_ref, qr_ref, lat_ref, kr_ref, o_ref):
    qi = pl.program_id(1)
    m_rows = N_HEADS * tq
    kv_lora = ql_ref.shape[-1]
    ql = ql_ref[...].reshape(m_rows, kv_lora).astype(BF16)
    qr = qr_ref[...].reshape(m_rows, DH_ROPE).astype(BF16)
    scale = (DH_NOPE + DH_ROPE) ** -0.5
    t_pos = qi * tq + lax.broadcasted_iota(jnp.int32, (m_rows, tk), 0) % tq
    col = lax.broadcasted_iota(jnp.int32, (m_rows, tk), 1)

    def step(j, carry, masked):
        start = pl.multiple_of(j * tk, tk)
        lat = lat_ref[pl.ds(start, tk), :].astype(BF16)
        kr = kr_ref[pl.ds(start, tk), :].astype(BF16)
        s = (_dot_nt(ql, lat) + _dot_nt(qr, kr)) * scale
        if masked:
            s = jnp.where(j * tk + col <= t_pos, s, NEG_BIG)
        return _softmax_step(carry, s, lambda p: _dot(p, lat))

    init = (jnp.full((m_rows, 1), NEG_BIG, F32), jnp.zeros((m_rows, 1), F32), jnp.zeros((m_rows, kv_lora), F32))
    n_full = (qi * tq) // tk
    carry = lax.fori_loop(0, n_full, lambda j, c: step(j, c, False), init)
    _, l, acc = step(n_full, carry, True)
    o_ref[...] = (acc / l).reshape(N_HEADS, tq, kv_lora)


def _mla_prompt(n_seq, seq_len, ql, qr, lat, kr):
    kv_lora = lat.shape[1]
    tq, tk = MLA_TQ, MLA_TK
    nq = seq_len // tq
    return pl.pallas_call(
        functools.partial(_mla_prompt_body, tq, tk),
        grid=(n_seq, nq),
        in_specs=[pl.BlockSpec((None, N_HEADS, tq, kv_lora), lambda b, i: (b, 0, i, 0)),
                  pl.BlockSpec((None, N_HEADS, tq, DH_ROPE), lambda b, i: (b, 0, i, 0)),
                  pl.BlockSpec((seq_len, kv_lora), lambda b, i: (b, 0)),
                  pl.BlockSpec((seq_len, DH_ROPE), lambda b, i: (b, 0))],
        out_specs=pl.BlockSpec((None, N_HEADS, tq, kv_lora), lambda b, i: (b, 0, i, 0)),
        out_shape=jax.ShapeDtypeStruct(ql.shape, F32),
        compiler_params=_params(2),
        name="mla_prompt_attention",
    )(ql, qr, lat, kr)


def _mla_sample_body(pps, nc, dt, pt_ref, ql_ref, qr_ref, latn_ref, krn_ref, *rest):
    lat_refs = rest[:pps]
    kr_refs = rest[pps:2 * pps]
    o_ref, m_ref, l_ref, acc_ref = rest[2 * pps:]
    c = pl.program_id(1)
    nrow = N_HEADS * dt
    kv_lora = ql_ref.shape[-1]
    scale = (DH_NOPE + DH_ROPE) ** -0.5

    @pl.when(c == 0)
    def _():
        m_ref[...] = jnp.full(m_ref.shape, NEG_BIG, F32)
        l_ref[...] = jnp.zeros(l_ref.shape, F32)
        acc_ref[...] = jnp.zeros(acc_ref.shape, F32)

    ql = ql_ref[...].reshape(nrow, kv_lora).astype(BF16)
    qr = qr_ref[...].reshape(nrow, DH_ROPE).astype(BF16)

    def update(s, pv):
        carry = _softmax_step((m_ref[...], l_ref[...], acc_ref[...]), s, pv)
        m_ref[...], l_ref[...], acc_ref[...] = carry

    lats = [r[...].astype(BF16) for r in lat_refs]
    s = jnp.concatenate([_dot_nt(ql, lats[i]) + _dot_nt(qr, kr_refs[i][...].astype(BF16))
                         for i in range(pps)], axis=1) * scale

    def pv_past(p):
        out = _dot(p[:, 0:PAGE_ROWS], lats[0])
        for i in range(1, pps):
            out = out + _dot(p[:, i * PAGE_ROWS:(i + 1) * PAGE_ROWS], lats[i])
        return out

    update(s, pv_past)

    @pl.when(c == nc - 1)
    def _():
        latn = _pad_rows(latn_ref[...], PAGE_ROWS).astype(BF16)
        krn = _pad_rows(krn_ref[...], PAGE_ROWS).astype(BF16)
        sn = (_dot_nt(ql, latn) + _dot_nt(qr, krn)) * scale
        t = lax.broadcasted_iota(jnp.int32, (nrow, PAGE_ROWS), 0) % dt
        key = lax.broadcasted_iota(jnp.int32, (nrow, PAGE_ROWS), 1)
        sn = jnp.where(key <= t, sn, NEG_BIG)
        update(sn, lambda p: _dot(p, latn))
        o_ref[...] = (acc_ref[...] / l_ref[...]).reshape(N_HEADS, dt, kv_lora)


def _mla_sample(layer, page_table, ql, qr, lat_new, kr_new, cache_lat, cache_kr):
    db, n_pages = page_table.shape
    dt = ql.shape[2]
    kv_lora = ql.shape[3]
    pps = PAGES_PER_STEP
    nc = n_pages // pps
    nrow = N_HEADS * dt
    head = lambda cdim: pl.BlockSpec((None, N_HEADS, dt, cdim), lambda b, ch, pt: (b, 0, 0, 0))
    tok = lambda cdim: pl.BlockSpec((dt, cdim), lambda b, ch, pt: (b, 0))
    grid_spec = pltpu.PrefetchScalarGridSpec(
        num_scalar_prefetch=1,
        grid=(db, nc),
        in_specs=[head(kv_lora), head(DH_ROPE), tok(kv_lora), tok(DH_ROPE)]
        + _page_specs(pps, layer, n_pages, pps, PAGE_ROWS, kv_lora)
        + _page_specs(pps, layer, n_pages, pps, PAGE_ROWS, DH_ROPE),
        out_specs=head(kv_lora),
        scratch_shapes=[pltpu.VMEM((nrow, 1), F32), pltpu.VMEM((nrow, 1), F32), pltpu.VMEM((nrow, kv_lora), F32)],
    )
    return pl.pallas_call(
        functools.partial(_mla_sample_body, pps, nc, dt),
        grid_spec=grid_spec,
        out_shape=jax.ShapeDtypeStruct(ql.shape, F32),
        compiler_params=_params(2),
        name="mla_sample_attention",
    )(page_table.reshape(-1), ql, qr, lat_new, kr_new, *([cache_lat] * pps), *([cache_kr] * pps))


def _odd_out_body(til, ol_ref, ag_ref, st_ref, x_ref, gate_ref, ng_ref, wuv_ref, woc_ref, wod_ref,
                  cw_ref, cb_ref, lg_ref, lb_ref, xo_ref, xp_ref, u_ref):
    g, r = til.groups, til.rows
    tm = g * r
    kv_lora = ol_ref.shape[-1]
    d_d = ag_ref.shape[-1]
    o_all = jnp.concatenate([ol_ref[:, hd].reshape(tm, kv_lora) for hd in range(N_HEADS)], axis=1).astype(BF16)
    yc = _dot(o_all, wuv_ref[...])

    lo = _stage_conv_input(xp_ref, ag_ref[...], st_ref, CONV_D_HALO, til)
    if g == 1:
        for r0 in range(0, r, CONV_CHUNK_ROWS):
            u_ref[:, r0:r0 + CONV_CHUNK_ROWS, :] = _conv_taps(xp_ref, cw_ref, lo, CONV_D_HALO, slice(None), r0,
                                                               CONV_CHUNK_ROWS)
    else:
        gchunk = CONV_CHUNK_ROWS // r
        for g0 in range(0, g, gchunk):
            u_ref[g0:g0 + gchunk] = _conv_taps(xp_ref, cw_ref, lo, CONV_D_HALO, slice(g0, g0 + gchunk), 0, r)
    u = u_ref[...].reshape(tm, d_d) + cb_ref[...]
    mu = jnp.mean(u, axis=-1, keepdims=True)
    var = jnp.mean(jnp.square(u - mu), axis=-1, keepdims=True)
    ln = (u - mu) * lax.rsqrt(var + EPS) * lg_ref[...] + lb_ref[...]
    yd = ln * jax.nn.sigmoid(ln)
    y = _dot(yc.astype(BF16), woc_ref[...]) + _dot(yd.astype(BF16), wod_ref[...])
    xo_ref[...] = x_ref[...] + gate_ref[...] * _rms(y.reshape(g, r, y.shape[-1]), ng_ref[...])


def _odd_out(til, ol, ag, state, x, mod, m, ng, wuv, woc, wod, cw, cb, lg, lb):
    n, d = x.shape
    d_d = ag.shape[1]
    kv_lora = ol.shape[-1]
    x3 = x.reshape(n // til.rows, til.rows, d)
    ag3 = ag.reshape(n // til.rows, til.rows, d_d)
    hp = _round_up(CONV_D_HALO, SUBLANES)
    out = pl.pallas_call(
        functools.partial(_odd_out_body, til),
        grid=(til.n_tiles,),
        in_specs=[_head_spec(til, kv_lora), _x_spec(til, d_d), _state_spec(til, CONV_D_HALO, d_d),
                  _x_spec(til, d), _mod_spec(til, m, 2, d), _const_spec(ng), _const_spec(wuv), _const_spec(woc),
                  _const_spec(wod), _const_spec(cw), _const_spec(cb), _const_spec(lg), _const_spec(lb)],
        out_specs=_x_spec(til, d),
        out_shape=jax.ShapeDtypeStruct(x3.shape, F32),
        scratch_shapes=[pltpu.VMEM((til.groups, hp + til.rows, d_d), F32),
                        pltpu.VMEM((til.groups, til.rows, d_d), F32)],
        compiler_params=_params(1),
        name="odd_out",
    )(ol, ag3, state, x3, mod, ng, wuv, woc, wod, cw, cb, lg, lb)
    return out.reshape(n, d)


def _ffn_body(til, n_chunks, x_ref, sh_ref, sc_ref, gate_ref, ngi_ref, ngo_ref, wup_ref, cw_ref, cb_ref, wdn_ref,
              st_ref, xo_ref, so_ref, xp_ref):
    g, r = til.groups, til.rows
    tm = g * r
    x = x_ref[...]
    d = x.shape[-1]
    d_ff = cw_ref.shape[-1]
    cwid = d_ff // n_chunks
    h = _rms(x, ngi_ref[...]) * (1.0 + sc_ref[...]) + sh_ref[...]
    h2 = h.reshape(tm, d).astype(BF16)
    y = None
    for c in range(n_chunks):
        csl = slice(c * cwid, (c + 1) * cwid)
        u = _dot(h2, wup_ref[:, csl]).reshape(g, r, cwid)
        gt = _dot(h2, wup_ref[:, d_ff + c * cwid:d_ff + (c + 1) * cwid])
        xpc = xp_ref.at[c]
        lo = _stage_conv_input(xpc, u, st_ref.at[:, :, csl], CONV_FFN_HALO, til)
        v = _conv_taps(xpc, cw_ref.at[:, csl], lo, CONV_FFN_HALO, slice(None), 0, r).reshape(tm, cwid)
        v = v + cb_ref[:, csl]
        act = (v * jax.nn.sigmoid(v) * gt).astype(BF16)
        part = _dot(act, wdn_ref[csl, :])
        y = part if y is None else y + part
        so_ref[:, :, csl] = xpc[:, lo + r:lo + r + CONV_FFN_HALO, :]
    xo_ref[...] = x + gate_ref[...] * _rms(y.reshape(g, r, d), ngo_ref[...])


def _ffn(til, n_seq, x, mod, m, ngi, ngo, wup, cw, cb, wdn, state):
    n, d = x.shape
    d_ff = cw.shape[1]
    n_chunks = 2
    x3 = x.reshape(n // til.rows, til.rows, d)
    out, st = pl.pallas_call(
        functools.partial(_ffn_body, til, n_chunks),
        grid=(til.n_tiles,),
        in_specs=[_x_spec(til, d), _mod_spec(til, m, 0, d), _mod_spec(til, m, 1, d), _mod_spec(til, m, 2, d),
                  _const_spec(ngi), _const_spec(ngo), _const_spec(wup), _const_spec(cw), _const_spec(cb),
                  _const_spec(wdn), _state_spec(til, CONV_FFN_HALO, d_ff)],
        out_specs=[_x_spec(til, d), _state_spec(til, CONV_FFN_HALO, d_ff)],
        out_shape=[jax.ShapeDtypeStruct(x3.shape, F32), jax.ShapeDtypeStruct((n_seq, CONV_FFN_HALO, d_ff), F32)],
        scratch_shapes=[pltpu.VMEM((n_chunks, til.groups, SUBLANES + til.rows, d_ff // n_chunks), F32)],
        compiler_params=_params(1),
        name="conv_ffn",
    )(x3, mod, mod, mod, ngi, ngo, wup, cw, cb, wdn, state)
    return out.reshape(n, d), st


def _rope_tables(pos):
    half = DH_ROPE // 2
    inv = ROPE_BASE ** (-jnp.arange(half, dtype=F32) / half)
    ang = pos.astype(F32)[:, None] * inv[None, :]
    cos, sin = jnp.cos(ang), jnp.sin(ang)
    n = pos.shape[0]
    pad = jnp.zeros((n, LANES - DH_ROPE), F32)
    return (jnp.tile(cos, (1, N_HEADS)), jnp.tile(sin, (1, N_HEADS)),
            jnp.concatenate([cos, cos, pad], axis=1), jnp.concatenate([sin, sin, pad], axis=1))


def _prep_even(even_w_in, conv_a_w, fox_b_f, even_w_out, j):
    d_a = conv_a_w.shape[-1]
    hb = N_HEADS * DH_FOX
    n_main = 3 * d_a + 3 * hb
    w = even_w_in[j]
    d = w.shape[0]
    w_main = w[:, :n_main].astype(BF16)
    w_fl = jnp.concatenate([w[:, n_main:], jnp.zeros((d, LANES - N_HEADS), F32)], axis=1).astype(BF16)
    b_f = jnp.concatenate([fox_b_f[j], jnp.zeros((LANES - N_HEADS,), F32)])[None, :]
    wo = even_w_out[j].astype(BF16)
    return w_main, w_fl, b_f, conv_a_w[j], wo[:d_a], wo[d_a:]


def _prep_odd(odd_w_in, mla_q_norm, mla_w_uq, mla_kv_norm, mla_w_ukv, odd_w_out, j):
    q_lora = mla_q_norm.shape[-1]
    kv_lora = mla_kv_norm.shape[-1]
    w = odd_w_in[j]
    d = w.shape[0]
    o_kr = q_lora + kv_lora
    o_glu = o_kr + DH_ROPE
    w_main = jnp.concatenate([w[:, :o_kr], w[:, o_glu:]], axis=1).astype(BF16)
    w_kr = jnp.concatenate([w[:, o_kr:o_glu], jnp.zeros((d, LANES - DH_ROPE), F32)], axis=1).astype(BF16)
    dq = DH_NOPE + DH_ROPE
    half = DH_ROPE // 2
    hh = jnp.arange(N_HEADS)[:, None]
    nope_cols = (hh * dq + jnp.arange(DH_NOPE)[None, :]).reshape(-1)
    r1_cols = (hh * dq + DH_NOPE + jnp.arange(half)[None, :]).reshape(-1)
    r2_cols = (hh * dq + DH_NOPE + half + jnp.arange(half)[None, :]).reshape(-1)
    wuq = mla_w_uq[j][:, jnp.concatenate([nope_cols, r1_cols, r2_cols])].astype(BF16)
    w_ukv = mla_w_ukv[j].reshape(kv_lora, N_HEADS, DH_NOPE + DH_V)
    per = LANES // DH_NOPE
    wuk = jnp.transpose(w_ukv[..., :DH_NOPE], (1, 2, 0)).reshape(N_HEADS // per, LANES, kv_lora).astype(BF16)
    wuv_h = jnp.transpose(w_ukv[..., DH_NOPE:], (1, 0, 2))
    eye = jnp.eye(N_HEADS, dtype=F32)
    wuv = (wuv_h[:, :, None, :] * eye[:, None, :, None]).reshape(N_HEADS * kv_lora, N_HEADS * DH_V).astype(BF16)
    src = jnp.arange(2 * LANES)
    sel = jnp.stack([
        ((src[:, None] == jnp.where(jnp.arange(DH_ROPE) < half, hd * half + jnp.arange(DH_ROPE),
                                    LANES + hd * half + jnp.arange(DH_ROPE) - half)[None, :]))
        for hd in range(N_HEADS)]).astype(BF16)
    wo = odd_w_out[j].astype(BF16)
    n_c = N_HEADS * DH_V
    return w_main, w_kr, mla_q_norm[j][None, :], mla_kv_norm[j][None, :], wuq, wuk, sel, wuv, wo[:n_c], wo[n_c:]


def _run_trunk(x3d, mod, til, tabs, prm, states, paged):
    n_seq, seq_len, d = x3d.shape
    x = x3d.reshape(n_seq * seq_len, d)
    conv_a_in, conv_d_in, ffn_in = states
    depth = prm["norm_g"].shape[0]
    fk, fv, flf, lat_o, kro_o, ca, cd, cf = [], [], [], [], [], [], [], []
    for i in range(depth):
        j = i // 2
        ng = prm["norm_g"][i]
        if i % 2 == 0:
            w_main, w_fl, b_f, cw, wo_a, wo_b = prm["even"][j]
            res = _even_in(til, n_seq, seq_len, x, mod, 2 * i, ng[0:1], w_main, w_fl, b_f, cw, conv_a_in[j],
                           want_lft=paged is None)
            if paged is None:
                ya, q, k, v, lf, lft, st = res
                yb = _fox_prompt(n_seq, seq_len, q, k, v, lft)
            else:
                ya, q, k, v, lf, st = res
                yb = _fox_sample(j, paged["page_table"], q, k, v, lf, paged["fox_k"], paged["fox_v"],
                                 paged["fox_logf"])
            x = _out_res(til, ya, yb, wo_a, wo_b, x, mod, 2 * i, ng[1:2])
            ca.append(st)
            fk.append(k.reshape(n_seq, seq_len, N_HEADS, DH_FOX))
            fv.append(v.reshape(n_seq, seq_len, N_HEADS, DH_FOX))
            flf.append(lf.reshape(n_seq, seq_len, N_HEADS))
        else:
            w_main, w_kr, qn, kvn, wuq, wuk, sel, wuv, wo_c, wo_d = prm["odd"][j]
            ql, qr, lat, kro, ag = _odd_in(til, n_seq, seq_len, x, mod, 2 * i, ng[0:1], w_main, w_kr, qn, kvn,
                                           wuq, wuk, sel, tabs)
            if paged is None:
                ol = _mla_prompt(n_seq, seq_len, ql, qr, lat, kro)
            else:
                ol = _mla_sample(j, paged["page_table"], ql, qr, lat, kro, paged["mla_latent"], paged["mla_krope"])
            x = _odd_out(til, ol, ag, conv_d_in[j], x, mod, 2 * i, ng[1:2], wuv, wo_c, wo_d,
                         prm["conv_d_w"][j], prm["conv_d_b"][j][None, :], prm["conv_d_ln_g"][j][None, :],
                         prm["conv_d_ln_b"][j][None, :])
            d_d = ag.shape[1]
            xp = jnp.concatenate([conv_d_in[j], ag.reshape(n_seq, seq_len, d_d)], axis=1)
            cd.append(xp[:, xp.shape[1] - CONV_D_HALO:])
            lat_o.append(lat.reshape(n_seq, seq_len, -1))
            kro_o.append(kro.reshape(n_seq, seq_len, DH_ROPE))
        x, st = _ffn(til, n_seq, x, mod, 2 * i + 1, ng[2:3], ng[3:4], prm["ffn_up"][i], prm["ffn_conv_w"][i],
                     prm["ffn_conv_b"][i][None, :], prm["ffn_down"][i], ffn_in[i])
        cf.append(st)
    return (x.reshape(n_seq, seq_len, d), jnp.stack(fk), jnp.stack(fv), jnp.stack(flf), jnp.stack(lat_o),
            jnp.stack(kro_o), jnp.stack(ca), jnp.stack(cd), jnp.stack(cf))


def kernel(x_prompt, x_sample, cache_fox_k, cache_fox_v, cache_fox_logf, cache_mla_latent, cache_mla_krope, state_conv_a, state_conv_d, state_ffn_conv, page_table, c_prompt, c_sample, ada_w, ada_b, norm_g, even_w_in, conv_a_w, fox_b_f, even_w_out, odd_w_in, mla_q_norm, mla_w_uq, mla_kv_norm, mla_w_ukv, conv_d_w, conv_d_b, conv_d_ln_g, conv_d_ln_b, odd_w_out, ffn_w_up, ffn_conv_w, ffn_conv_b, ffn_w_down):
    b, t, d = x_prompt.shape
    db, dt, _ = x_sample.shape
    depth = norm_g.shape[0]
    n_even, n_odd = (depth + 1) // 2, depth // 2
    assert dt == SUBLANES and t % PROMPT_TILE == 0 and (db * dt) % SAMPLE_TILE == 0
    assert t % FOX_TQ == 0 and t % MLA_TK == 0 and page_table.shape[1] % PAGES_PER_STEP == 0

    prm = dict(
        norm_g=norm_g,
        even=[_prep_even(even_w_in, conv_a_w, fox_b_f, even_w_out, j) for j in range(n_even)],
        odd=[_prep_odd(odd_w_in, mla_q_norm, mla_w_uq, mla_kv_norm, mla_w_ukv, odd_w_out, j) for j in range(n_odd)],
        conv_d_w=conv_d_w, conv_d_b=conv_d_b, conv_d_ln_g=conv_d_ln_g, conv_d_ln_b=conv_d_ln_b,
        ffn_up=ffn_w_up.astype(BF16), ffn_conv_w=ffn_conv_w, ffn_conv_b=ffn_conv_b, ffn_down=ffn_w_down.astype(BF16),
    )
    mod_p, mod_s = _modulation(c_prompt, c_sample, ada_w, ada_b)

    til_p = _Tiling(1, PROMPT_TILE, t // PROMPT_TILE, b * t // PROMPT_TILE)
    til_s = _Tiling(SAMPLE_TILE // dt, dt, 1, db * dt // SAMPLE_TILE)
    d_a, d_d, d_ff = conv_a_w.shape[-1], conv_d_w.shape[-1], ffn_conv_w.shape[-1]
    zero_states = (jnp.zeros((n_even, b, CONV_A_HALO, d_a), F32), jnp.zeros((n_odd, b, CONV_D_HALO, d_d), F32),
                   jnp.zeros((depth, b, CONV_FFN_HALO, d_ff), F32))
    past_len = page_table.shape[1] * PAGE_ROWS
    paged = dict(page_table=page_table, fox_k=cache_fox_k, fox_v=cache_fox_v, fox_logf=cache_fox_logf,
                 mla_latent=cache_mla_latent, mla_krope=cache_mla_krope)
    out_p = _run_trunk(x_prompt, mod_p, til_p, _rope_tables(jnp.arange(t, dtype=jnp.int32)), prm, zero_states, None)
    out_s = _run_trunk(x_sample, mod_s, til_s, _rope_tables(past_len + jnp.arange(dt, dtype=jnp.int32)), prm,
                       (state_conv_a, state_conv_d, state_ffn_conv), paged)
    return (out_p[0], out_s[0]) + tuple(out_p[1:]) + tuple(out_s[1:])
```

```python
import functools
from typing import NamedTuple

import jax
import jax.numpy as jnp
from jax import lax
from jax.experimental import pallas as pl
from jax.experimental.pallas import tpu as pltpu

F32 = jnp.float32
BF16 = jnp.bfloat16

EPS = 1e-6
PAGE_ROWS = 128
N_HEADS = 8
DH_FOX = 64
DH_NOPE = 64
DH_ROPE = 32
DH_V = 64
ROPE_BASE = 10000.0
CONV_A_HALO = 2
CONV_D_HALO = 30
CONV_FFN_HALO = 2
LANES = 128
SUBLANES = 8
NEG_BIG = -1e30
VMEM_LIMIT = 56 * 1024 * 1024
PROMPT_TILE = 512
SAMPLE_TILE = 512
FOX_TQ = 256
MLA_TQ = 128
MLA_TK = 256
FOX_PAGES_PER_STEP = 16
MLA_PAGES_PER_STEP = 32
CONV_CHUNK_ROWS = 64


class _Tiling(NamedTuple):
    groups: int
    rows: int
    tiles_per_seq: int
    n_tiles: int


def _round_up(x, m):
    return (x + m - 1) // m * m


def _dot(a, b):
    return jnp.dot(a, b, preferred_element_type=F32)


def _dot_nt(a, b):
    return lax.dot_general(a, b, (((1,), (1,)), ((), ())), preferred_element_type=F32)


def _rms(x, g):
    return x * lax.rsqrt(jnp.mean(x * x, axis=-1, keepdims=True) + EPS) * g


def _params(n_axes=1):
    return pltpu.CompilerParams(dimension_semantics=("arbitrary",) * n_axes,
                                vmem_limit_bytes=VMEM_LIMIT)


def _const_spec(arr):
    nd = arr.ndim
    return pl.BlockSpec(arr.shape, lambda *_: (0,) * nd, pipeline_mode=pl.Buffered(1))


def _mod_spec(til, m, comp, d):
    return pl.BlockSpec((None, til.groups, 1, d), lambda i: (m, i // til.tiles_per_seq, 0, comp))


def _x_spec(til, c):
    return pl.BlockSpec((til.groups, til.rows, c), lambda i: (i, 0, 0))


def _row_spec(til, c):
    return pl.BlockSpec((til.groups * til.rows, c), lambda i: (i, 0))


def _state_spec(til, halo, c):
    return pl.BlockSpec((til.groups, halo, c), lambda i: (i // til.tiles_per_seq, 0, 0))


def _head_spec(til, c):
    return pl.BlockSpec((til.groups, N_HEADS, til.rows, c),
                        lambda i: (i // til.tiles_per_seq, 0, i % til.tiles_per_seq, 0))


def _tab_spec(til):
    return pl.BlockSpec((til.rows, LANES), lambda i: (i % til.tiles_per_seq, 0))


def _stage_conv_input(xp_ref, new, st_ref, halo, til):
    rows = til.rows
    hp = _round_up(halo, SUBLANES)
    lo = hp - halo
    if til.tiles_per_seq > 1:
        first = (pl.program_id(0) % til.tiles_per_seq) == 0

        @pl.when(first)
        def _():
            xp_ref[:, lo:hp, :] = st_ref[...]

        @pl.when(jnp.logical_not(first))
        def _():
            xp_ref[:, lo:hp, :] = xp_ref[:, lo + rows:hp + rows, :]
    else:
        xp_ref[:, lo:hp, :] = st_ref[...]
    xp_ref[:, hp:hp + rows, :] = new
    return lo


def _conv_taps(xp_ref, w_ref, lo, halo, gsl, r0, nrows):
    acc = w_ref[0:1, :] * xp_ref[gsl, lo + r0:lo + r0 + nrows, :]
    for k in range(1, halo + 1):
        acc = acc + w_ref[k:k + 1, :] * xp_ref[gsl, lo + r0 + k:lo + r0 + k + nrows, :]
    return acc


def _mod_body(cp_ref, cs_ref, w_ref, b_ref, op_ref, os_ref):
    w = w_ref[...].astype(BF16)
    b = b_ref[...]
    for c_ref, o_ref in ((cp_ref, op_ref), (cs_ref, os_ref)):
        c = c_ref[...]
        a = (c * jax.nn.sigmoid(c)).astype(BF16)
        o_ref[...] = _dot(a, w) + b


def _modulation(c_p, c_s, ada_w, ada_b):
    d = c_p.shape[1]
    n_mod = ada_w.shape[0] * ada_w.shape[1]
    w = ada_w.reshape(n_mod, d, 3 * d)
    b = ada_b.reshape(n_mod, 1, 3 * d)
    bp, bs = c_p.shape[0], c_s.shape[0]
    out_p, out_s = pl.pallas_call(
        _mod_body,
        grid=(n_mod, 3),
        in_specs=[pl.BlockSpec((bp, d), lambda m, n: (0, 0)),
                  pl.BlockSpec((bs, d), lambda m, n: (0, 0)),
                  pl.BlockSpec((None, d, d), lambda m, n: (m, 0, n)),
                  pl.BlockSpec((None, 1, d), lambda m, n: (m, 0, n))],
        out_specs=[pl.BlockSpec((None, bp, d), lambda m, n: (m, 0, n)),
                   pl.BlockSpec((None, bs, d), lambda m, n: (m, 0, n))],
        out_shape=[jax.ShapeDtypeStruct((n_mod, bp, 3 * d), F32),
                   jax.ShapeDtypeStruct((n_mod, bs, 3 * d), F32)],
        compiler_params=_params(2),
        name="adaln_modulation",
    )(c_p, c_s, w, b)
    return out_p.reshape(n_mod, bp, 1, 3 * d), out_s.reshape(n_mod, bs, 1, 3 * d)


def _even_in_body(til, want_lft, x_ref, sh_ref, sc_ref, ng_ref, w_ref, wfl_ref, bf_ref, cw_ref, st_ref,
                  ya_ref, q_ref, k_ref, v_ref, lf_ref, *rest):
    if want_lft:
        lft_ref, so_ref, xp_ref = rest
    else:
        so_ref, xp_ref = rest
    g, r = til.groups, til.rows
    tm = g * r
    x = x_ref[...]
    d = x.shape[-1]
    h = _rms(x, ng_ref[...]) * (1.0 + sc_ref[...]) + sh_ref[...]
    h2 = h.reshape(tm, d).astype(BF16)
    da = ya_ref.shape[-1]

    def proj(c):
        return _dot(h2, w_ref[:, c * da:(c + 1) * da])

    xa, gb, gc = proj(0), proj(1), proj(2)
    q_ref[...] = proj(3)
    k_ref[...] = proj(4)
    v_ref[...] = proj(5)
    lf = jax.nn.log_sigmoid(_dot(h2, wfl_ref[...]) + bf_ref[...])
    lf_ref[...] = lf[:, :N_HEADS]
    if want_lft:
        lft_ref[...] = lf.T[:N_HEADS, :]
    p = (gc * xa).reshape(g, r, da)
    lo = _stage_conv_input(xp_ref, p, st_ref, CONV_A_HALO, til)
    u = _conv_taps(xp_ref, cw_ref, lo, CONV_A_HALO, slice(None), 0, r)
    ya_ref[...] = gb * u.reshape(tm, da)
    so_ref[...] = xp_ref[:, lo + r:lo + r + CONV_A_HALO, :]


def _even_in(til, n_seq, seq_len, x, mod, m, ng, w_main, w_fl, b_f, conv_w, state, want_lft):
    n, d = x.shape
    da = conv_w.shape[1]
    tm = til.groups * til.rows
    x3 = x.reshape(n // til.rows, til.rows, d)
    out_shape = [jax.ShapeDtypeStruct((n, da), F32)] * 4 + [jax.ShapeDtypeStruct((n, N_HEADS), F32)]
    out_specs = [_row_spec(til, da)] * 4 + [_row_spec(til, N_HEADS)]
    if want_lft:
        out_shape.append(jax.ShapeDtypeStruct((n_seq, N_HEADS, seq_len), F32))
        out_specs.append(pl.BlockSpec((None, N_HEADS, til.rows),
                                      lambda i: (i // til.tiles_per_seq, 0, i % til.tiles_per_seq)))
    out_shape.append(jax.ShapeDtypeStruct((n_seq, CONV_A_HALO, da), F32))
    out_specs.append(_state_spec(til, CONV_A_HALO, da))
    return pl.pallas_call(
        functools.partial(_even_in_body, til, want_lft),
        grid=(til.n_tiles,),
        in_specs=[_x_spec(til, d), _mod_spec(til, m, 0, d), _mod_spec(til, m, 1, d), _const_spec(ng),
                  _const_spec(w_main), _const_spec(w_fl), _const_spec(b_f), _const_spec(conv_w),
                  _state_spec(til, CONV_A_HALO, da)],
        out_specs=out_specs,
        out_shape=out_shape,
        scratch_shapes=[pltpu.VMEM((til.groups, SUBLANES + til.rows, da), F32)],
        compiler_params=_params(1),
        name="even_in",
    )(x3, mod, mod, ng, w_main, w_fl, b_f, conv_w, state)


def _out_res_body(til, a1_ref, a2_ref, w1_ref, w2_ref, x_ref, gate_ref, ng_ref, xo_ref):
    g, r = til.groups, til.rows
    y = _dot(a1_ref[...].astype(BF16), w1_ref[...]) + _dot(a2_ref[...].astype(BF16), w2_ref[...])
    y3 = y.reshape(g, r, y.shape[-1])
    xo_ref[...] = x_ref[...] + gate_ref[...] * _rms(y3, ng_ref[...])


def _out_res(til, a1, a2, w1, w2, x, mod, m, ng):
    n, d = x.shape
    x3 = x.reshape(n // til.rows, til.rows, d)
    out = pl.pallas_call(
        functools.partial(_out_res_body, til),
        grid=(til.n_tiles,),
        in_specs=[_row_spec(til, a1.shape[1]), _row_spec(til, a2.shape[1]), _const_spec(w1), _const_spec(w2),
                  _x_spec(til, d), _mod_spec(til, m, 2, d), _const_spec(ng)],
        out_specs=_x_spec(til, d),
        out_shape=jax.ShapeDtypeStruct(x3.shape, F32),
        compiler_params=_params(1),
        name="even_out",
    )(a1, a2, w1, w2, x3, mod, ng)
    return out.reshape(n, d)


def _cumsum_lanes(x):
    n = x.shape[-1]
    idx = lax.broadcasted_iota(jnp.int32, x.shape, x.ndim - 1)
    s = 1
    while s < n:
        x = x + jnp.where(idx >= s, pltpu.roll(x, s, axis=x.ndim - 1), 0.0)
        s *= 2
    return x


def _softmax_step(carry, s, pv):
    m, l, acc = carry
    m_new = jnp.maximum(m, jnp.max(s, axis=1, keepdims=True))
    alpha = jnp.exp(m - m_new)
    p = jnp.exp(s - m_new)
    l = alpha * l + jnp.sum(p, axis=1, keepdims=True)
    acc = alpha * acc + pv(p.astype(BF16))
    return m_new, l, acc


def _fox_prompt_body(tq, nq, q_ref, k_ref, v_ref, lft_ref, o_ref, cum_ref):
    qi = pl.program_id(1)

    @pl.when(qi == 0)
    def _():
        c = _cumsum_lanes(lft_ref[...])
        for j in range(nq):
            cum_ref[j] = c[:, j * tq:(j + 1) * tq]

    scale = DH_FOX ** -0.5
    lane = lax.broadcasted_iota(jnp.int32, (tq, LANES), 1)
    row = lax.broadcasted_iota(jnp.int32, (tq, tq), 0)
    col = lax.broadcasted_iota(jnp.int32, (tq, tq), 1)
    per = LANES // DH_FOX
    n_cg = N_HEADS // per
    qms = []
    for cg in range(n_cg):
        qc = q_ref[:, cg * LANES:(cg + 1) * LANES] * scale
        for e in range(per):
            qms.append(jnp.where(lane // DH_FOX == e, qc, 0.0).astype(BF16))

    def block(j, carry, masked):
        start = pl.multiple_of(j * tq, tq)
        cum_j = cum_ref[j]
        new = []
        for cg in range(n_cg):
            csl = slice(cg * LANES, (cg + 1) * LANES)
            kb = k_ref[pl.ds(start, tq), csl].astype(BF16)
            vb = v_ref[pl.ds(start, tq), csl].astype(BF16)
            for e in range(per):
                h = cg * per + e
                s = _dot_nt(qms[h], kb) - cum_j[h:h + 1, :]
                if masked:
                    s = jnp.where(col <= row, s, NEG_BIG)
                new.append(_softmax_step(carry[h], s, lambda p, vb=vb: _dot(p, vb)))
        return tuple(new)

    init = tuple((jnp.full((tq, 1), NEG_BIG, F32), jnp.zeros((tq, 1), F32), jnp.zeros((tq, LANES), F32))
                 for _ in range(N_HEADS))
    carry = lax.fori_loop(0, qi, lambda j, c: block(j, c, False), init)
    carry = block(qi, carry, True)
    for cg in range(n_cg):
        outs = [carry[cg * per + e][2] / carry[cg * per + e][1] for e in range(per)]
        o_ref[:, cg * LANES:(cg + 1) * LANES] = jnp.where(lane < DH_FOX, outs[0], outs[1])


def _fox_prompt(n_seq, seq_len, q, k, v, lft):
    n, c = q.shape
    tq = FOX_TQ
    nq = seq_len // tq
    return pl.pallas_call(
        functools.partial(_fox_prompt_body, tq, nq),
        grid=(n_seq, nq),
        in_specs=[pl.BlockSpec((tq, c), lambda b, i: (b * nq + i, 0)),
                  pl.BlockSpec((seq_len, c), lambda b, i: (b, 0)),
                  pl.BlockSpec((seq_len, c), lambda b, i: (b, 0)),
                  pl.BlockSpec((None, N_HEADS, seq_len), lambda b, i: (b, 0, 0))],
        out_specs=pl.BlockSpec((tq, c), lambda b, i: (b * nq + i, 0)),
        out_shape=jax.ShapeDtypeStruct((n, c), F32),
        scratch_shapes=[pltpu.VMEM((nq, N_HEADS, tq), F32)],
        compiler_params=_params(2),
        name="fox_prompt_attention",
    )(q, k, v, lft)


def _rows_to_lanes(x, pad_ref):
    pad_ref[...] = jnp.zeros(pad_ref.shape, F32)
    pad_ref[0:x.shape[0], 0:x.shape[1]] = x
    return pad_ref[...].T[0:x.shape[1], :]


def _pad_rows(x, rows):
    return jnp.concatenate([x, jnp.zeros((rows - x.shape[0], x.shape[1]), x.dtype)], axis=0)


def _split3(x):
    hi = x.astype(BF16)
    r1 = x - hi.astype(F32)
    mid = r1.astype(BF16)
    lo = (r1 - mid.astype(F32)).astype(BF16)
    return hi, mid, lo


def _dot_f32_by_01(x, w01):
    hi, mid, lo = _split3(x)
    return _dot(hi, w01) + _dot(mid, w01) + _dot(lo, w01)


def _fox_sample_body(pps, nc, dt, pt_ref, q_ref, kn_ref, vn_ref, lfn_ref, tri_ref, *rest):
    k_refs = rest[:pps]
    v_refs = rest[pps:2 * pps]
    lf_refs = rest[2 * pps:3 * pps]
    o_ref, m_ref, l_ref, acc_ref, pre_ref, pad_ref = rest[3 * pps:]
    c = pl.program_id(1)
    nrow = N_HEADS * dt
    width = N_HEADS * DH_FOX

    @pl.when(c == 0)
    def _():
        m_ref[...] = jnp.full(m_ref.shape, NEG_BIG, F32)
        l_ref[...] = jnp.zeros(l_ref.shape, F32)
        acc_ref[...] = jnp.zeros(acc_ref.shape, F32)
        pre_ref[...] = jnp.zeros(pre_ref.shape, F32)

    q = q_ref[...] * (DH_FOX ** -0.5)
    qt = jnp.broadcast_to(q[None], (N_HEADS, dt, width)).reshape(nrow, width)
    rh = lax.broadcasted_iota(jnp.int32, (nrow, width), 0) // dt
    ch = lax.broadcasted_iota(jnp.int32, (nrow, width), 1) // DH_FOX
    qbd = jnp.where(rh == ch, qt, 0.0).astype(BF16)

    def expand_heads(x):
        return jnp.broadcast_to(x[:, None, :], (N_HEADS, dt, x.shape[-1])).reshape(nrow, x.shape[-1])

    def update(s, pv):
        carry = _softmax_step((m_ref[...], l_ref[...], acc_ref[...]), s, pv)
        m_ref[...], l_ref[...], acc_ref[...] = carry

    x_all = jnp.concatenate([r[...] for r in lf_refs], axis=0)
    cw = _dot_f32_by_01(x_all, tri_ref[...])
    pre = pre_ref[...]
    cols = []
    for i in range(pps):
        rsl = slice(i * N_HEADS, (i + 1) * N_HEADS)
        cum_i = cw[rsl, 0:PAGE_ROWS] + pre
        pre = pre + cw[rsl, PAGE_ROWS:2 * PAGE_ROWS]
        cols.append(_dot(qbd, k_refs[i][...].astype(BF16)) - expand_heads(cum_i))
    pre_ref[...] = pre
    s = jnp.concatenate(cols, axis=1)

    def pv_past(p):
        out = _dot_nt(p[:, 0:PAGE_ROWS], v_refs[0][...].astype(BF16))
        for i in range(1, pps):
            out = out + _dot_nt(p[:, i * PAGE_ROWS:(i + 1) * PAGE_ROWS], v_refs[i][...].astype(BF16))
        return out

    update(s, pv_past)

    @pl.when(c == nc - 1)
    def _():
        lfn = _rows_to_lanes(lfn_ref[...], pad_ref)
        cumn = _dot_f32_by_01(lfn, tri_ref[:, 0:PAGE_ROWS]) + pre_ref[...]
        kn = _pad_rows(kn_ref[...], PAGE_ROWS).astype(BF16)
        vn = _pad_rows(vn_ref[...], PAGE_ROWS).astype(BF16)
        sn = _dot_nt(qbd, kn) - expand_heads(cumn)
        t = lax.broadcasted_iota(jnp.int32, (nrow, PAGE_ROWS), 0) % dt
        key = lax.broadcasted_iota(jnp.int32, (nrow, PAGE_ROWS), 1)
        sn = jnp.where(key <= t, sn, NEG_BIG)
        update(sn, lambda p: _dot(p, vn))
        o = (acc_ref[...] / l_ref[...]).reshape(N_HEADS, dt, width)
        hh = lax.broadcasted_iota(jnp.int32, (N_HEADS, dt, width), 0)
        cc = lax.broadcasted_iota(jnp.int32, (N_HEADS, dt, width), 2) // DH_FOX
        o_ref[...] = jnp.sum(jnp.where(hh == cc, o, 0.0), axis=0)


def _page_specs(n, layer, n_pages, pps, rows, c):
    return [pl.BlockSpec((None, None, rows, c),
                         lambda b, ch, pt, i=i: (layer, pt[b * n_pages + ch * pps + i], 0, 0))
            for i in range(n)]


def _key_minor(cache):
    nd = cache.ndim
    moved = jnp.transpose(cache, (0, 1) + tuple(range(3, nd)) + (2,))
    return moved.reshape(cache.shape[0], cache.shape[1], -1, cache.shape[2])


def _fox_sample(layer, page_table, q, k_new, v_new, lf_new, cache_kt, cache_vt, cache_lft):
    db, n_pages = page_table.shape
    n, width = q.shape
    dt = n // db
    pps = FOX_PAGES_PER_STEP
    nc = n_pages // pps
    nrow = N_HEADS * dt
    tri = jnp.concatenate([jnp.triu(jnp.ones((PAGE_ROWS, PAGE_ROWS), F32)), jnp.ones((PAGE_ROWS, PAGE_ROWS), F32)],
                          axis=1).astype(BF16)
    tok = lambda cdim: pl.BlockSpec((dt, cdim), lambda b, ch, pt: (b, 0))
    grid_spec = pltpu.PrefetchScalarGridSpec(
        num_scalar_prefetch=1,
        grid=(db, nc),
        in_specs=[tok(width), tok(width), tok(width), tok(N_HEADS),
                  pl.BlockSpec(tri.shape, lambda b, ch, pt: (0, 0))]
        + _page_specs(pps, layer, n_pages, pps, width, PAGE_ROWS)
        + _page_specs(pps, layer, n_pages, pps, width, PAGE_ROWS)
        + _page_specs(pps, layer, n_pages, pps, N_HEADS, PAGE_ROWS),
        out_specs=tok(width),
        scratch_shapes=[pltpu.VMEM((nrow, 1), F32), pltpu.VMEM((nrow, 1), F32), pltpu.VMEM((nrow, width), F32),
                        pltpu.VMEM((N_HEADS, LANES), F32), pltpu.VMEM((PAGE_ROWS, LANES), F32)],
    )
    return pl.pallas_call(
        functools.partial(_fox_sample_body, pps, nc, dt),
        grid_spec=grid_spec,
        out_shape=jax.ShapeDtypeStruct((n, width), F32),
        compiler_params=_params(2),
        name="fox_sample_attention",
    )(page_table.reshape(-1), q, k_new, v_new, lf_new, tri, *([cache_kt] * pps), *([cache_vt] * pps),
      *([cache_lft] * pps))


def _odd_in_body(til, x_ref, sh_ref, sc_ref, ng_ref, w_ref, wkr_ref, qn_ref, kvn_ref, wuq_ref, wuk_ref, sel_ref,
                 cos8_ref, sin8_ref, cosk_ref, sink_ref, ql_ref, qr_ref, lat_ref, kro_ref, ag_ref):
    g, r = til.groups, til.rows
    tm = g * r
    x = x_ref[...]
    d = x.shape[-1]
    h = _rms(x, ng_ref[...]) * (1.0 + sc_ref[...]) + sh_ref[...]
    h2 = h.reshape(tm, d).astype(BF16)
    q_lora = qn_ref.shape[-1]
    kv_lora = kvn_ref.shape[-1]
    d_d = ag_ref.shape[-1]
    o_kv = q_lora
    o_a = o_kv + kv_lora
    o_g = o_a + d_d

    def bcast_tab(ref):
        return jnp.broadcast_to(ref[...][None], (g, r, LANES)).reshape(tm, LANES)

    cq = _rms(_dot(h2, w_ref[:, 0:o_kv]), qn_ref[...]).astype(BF16)
    qq = _dot(cq, wuq_ref[...])
    n_nope = N_HEADS * DH_NOPE
    r1 = qq[:, n_nope:n_nope + LANES]
    r2 = qq[:, n_nope + LANES:n_nope + 2 * LANES]
    cos8, sin8 = bcast_tab(cos8_ref), bcast_tab(sin8_ref)
    qr_all = jnp.concatenate([r1 * cos8 - r2 * sin8, r1 * sin8 + r2 * cos8], axis=1).astype(BF16)
    lane = lax.broadcasted_iota(jnp.int32, (tm, LANES), 1)
    per = LANES // DH_NOPE
    for hd in range(N_HEADS):
        cg, e = hd // per, hd % per
        qn = jnp.where(lane // DH_NOPE == e, qq[:, cg * LANES:(cg + 1) * LANES], 0.0).astype(BF16)
        ql_ref[:, hd] = _dot(qn, wuk_ref[cg]).reshape(g, r, kv_lora)
        qr_ref[:, hd] = _dot(qr_all, sel_ref[hd]).reshape(g, r, DH_ROPE)

    lat_ref[...] = _rms(_dot(h2, w_ref[:, o_kv:o_a]), kvn_ref[...])
    kr = _dot(h2, wkr_ref[...])
    half = DH_ROPE // 2
    swapped = jnp.where(lane < half, -pltpu.roll(kr, LANES - half, axis=1), pltpu.roll(kr, half, axis=1))
    kro = kr * bcast_tab(cosk_ref) + swapped * bcast_tab(sink_ref)
    kro_ref[...] = kro[:, 0:DH_ROPE]

    a = _dot(h2, w_ref[:, o_a:o_g])
    gt = _dot(h2, w_ref[:, o_g:o_g + d_d])
    ag_ref[...] = a * jax.nn.sigmoid(gt)


def _odd_in(til, n_seq, seq_len, x, mod, m, ng, w_main, w_kr, qn, kvn, wuq, wuk, sel, tabs):
    n, d = x.shape
    x3 = x.reshape(n // til.rows, til.rows, d)
    kv_lora = kvn.shape[-1]
    d_d = (w_main.shape[1] - qn.shape[-1] - kv_lora) // 2
    cos8, sin8, cosk, sink = tabs
    return pl.pallas_call(
        functools.partial(_odd_in_body, til),
        grid=(til.n_tiles,),
        in_specs=[_x_spec(til, d), _mod_spec(til, m, 0, d), _mod_spec(til, m, 1, d), _const_spec(ng),
                  _const_spec(w_main), _const_spec(w_kr), _const_spec(qn), _const_spec(kvn), _const_spec(wuq),
                  _const_spec(wuk), _const_spec(sel),
                  _tab_spec(til), _tab_spec(til), _tab_spec(til), _tab_spec(til)],
        out_specs=[_head_spec(til, kv_lora), _head_spec(til, DH_ROPE), _row_spec(til, kv_lora),
                   _row_spec(til, DH_ROPE), _row_spec(til, d_d)],
        out_shape=[jax.ShapeDtypeStruct((n_seq, N_HEADS, seq_len, kv_lora), F32),
                   jax.ShapeDtypeStruct((n_seq, N_HEADS, seq_len, DH_ROPE), F32),
                   jax.ShapeDtypeStruct((n, kv_lora), F32),
                   jax.ShapeDtypeStruct((n, DH_ROPE), F32),
                   jax.ShapeDtypeStruct((n, d_d), F32)],
        compiler_params=_params(1),
        name="odd_in",
    )(x3, mod, mod, ng, w_main, w_kr, qn, kvn, wuq, wuk, sel, cos8, sin8, cosk, sink)


def _mla_prompt_body(tq, tk, ql_ref, qr_ref, lat_ref, kr_ref, o_ref):
    qi = pl.program_id(1)
    m_rows = N_HEADS * tq
    kv_lora = ql_ref.shape[-1]
    ql = ql_ref[...].reshape(m_rows, kv_lora).astype(BF16)
    qr = qr_ref[...].reshape(m_rows, DH_ROPE).astype(BF16)
    scale = (DH_NOPE + DH_ROPE) ** -0.5
    t_pos = qi * tq + lax.broadcasted_iota(jnp.int32, (m_rows, tk), 0) % tq
    col = lax.broadcasted_iota(jnp.int32, (m_rows, tk), 1)

    def step(j, carry, masked):
        start = pl.multiple_of(j * tk, tk)
        lat = lat_ref[pl.ds(start, tk), :].astype(BF16)
        kr = kr_ref[pl.ds(start, tk), :].astype(BF16)
        s = (_dot_nt(ql, lat) + _dot_nt(qr, kr)) * scale
        if masked:
            s = jnp.where(j * tk + col <= t_pos, s, NEG_BIG)
        return _softmax_step(carry, s, lambda p: _dot(p, lat))

    init = (jnp.full((m_rows, 1), NEG_BIG, F32), jnp.zeros((m_rows, 1), F32), jnp.zeros((m_rows, kv_lora), F32))
    n_full = (qi * tq) // tk
    carry = lax.fori_loop(0, n_full, lambda j, c: step(j, c, False), init)
    _, l, acc = step(n_full, carry, True)
    o_ref[...] = (acc / l).reshape(N_HEADS, tq, kv_lora)


def _mla_prompt(n_seq, seq_len, ql, qr, lat, kr):
    kv_lora = lat.shape[1]
    tq, tk = MLA_TQ, MLA_TK
    nq = seq_len // tq
    return pl.pallas_call(
        functools.partial(_mla_prompt_body, tq, tk),
        grid=(n_seq, nq),
        in_specs=[pl.BlockSpec((None, N_HEADS, tq, kv_lora), lambda b, i: (b, 0, i, 0)),
                  pl.BlockSpec((None, N_HEADS, tq, DH_ROPE), lambda b, i: (b, 0, i, 0)),
                  pl.BlockSpec((seq_len, kv_lora), lambda b, i: (b, 0)),
                  pl.BlockSpec((seq_len, DH_ROPE), lambda b, i: (b, 0))],
        out_specs=pl.BlockSpec((None, N_HEADS, tq, kv_lora), lambda b, i: (b, 0, i, 0)),
        out_shape=jax.ShapeDtypeStruct(ql.shape, F32),
        compiler_params=_params(2),
        name="mla_prompt_attention",
    )(ql, qr, lat, kr)


def _mla_sample_body(pps, nc, dt, pt_ref, ql_ref, qr_ref, latn_ref, krn_ref, *rest):
    lat_refs = rest[:pps]
    kr_refs = rest[pps:2 * pps]
    o_ref, m_ref, l_ref, acc_ref = rest[2 * pps:]
    c = pl.program_id(1)
    nrow = N_HEADS * dt
    kv_lora = ql_ref.shape[-1]
    scale = (DH_NOPE + DH_ROPE) ** -0.5

    @pl.when(c == 0)
    def _():
        m_ref[...] = jnp.full(m_ref.shape, NEG_BIG, F32)
        l_ref[...] = jnp.zeros(l_ref.shape, F32)
        acc_ref[...] = jnp.zeros(acc_ref.shape, F32)

    ql = ql_ref[...].reshape(nrow, kv_lora).astype(BF16)
    qr = qr_ref[...].reshape(nrow, DH_ROPE).astype(BF16)

    def update(s, pv):
        carry = _softmax_step((m_ref[...], l_ref[...], acc_ref[...]), s, pv)
        m_ref[...], l_ref[...], acc_ref[...] = carry

    lats = [r[...].astype(BF16) for r in lat_refs]
    s = jnp.concatenate([_dot_nt(ql, lats[i]) + _dot(qr, kr_refs[i][...].astype(BF16))
                         for i in range(pps)], axis=1) * scale

    def pv_past(p):
        out = _dot(p[:, 0:PAGE_ROWS], lats[0])
        for i in range(1, pps):
            out = out + _dot(p[:, i * PAGE_ROWS:(i + 1) * PAGE_ROWS], lats[i])
        return out

    update(s, pv_past)

    @pl.when(c == nc - 1)
    def _():
        latn = _pad_rows(latn_ref[...], PAGE_ROWS).astype(BF16)
        krn = _pad_rows(krn_ref[...], PAGE_ROWS).astype(BF16)
        sn = (_dot_nt(ql, latn) + _dot_nt(qr, krn)) * scale
        t = lax.broadcasted_iota(jnp.int32, (nrow, PAGE_ROWS), 0) % dt
        key = lax.broadcasted_iota(jnp.int32, (nrow, PAGE_ROWS), 1)
        sn = jnp.where(key <= t, sn, NEG_BIG)
        update(sn, lambda p: _dot(p, latn))
        o_ref[...] = (acc_ref[...] / l_ref[...]).reshape(N_HEADS, dt, kv_lora)


def _mla_sample(layer, page_table, ql, qr, lat_new, kr_new, cache_lat, cache_krt):
    db, n_pages = page_table.shape
    dt = ql.shape[2]
    kv_lora = ql.shape[3]
    pps = MLA_PAGES_PER_STEP
    nc = n_pages // pps
    nrow = N_HEADS * dt
    head = lambda cdim: pl.BlockSpec((None, N_HEADS, dt, cdim), lambda b, ch, pt: (b, 0, 0, 0))
    tok = lambda cdim: pl.BlockSpec((dt, cdim), lambda b, ch, pt: (b, 0))
    grid_spec = pltpu.PrefetchScalarGridSpec(
        num_scalar_prefetch=1,
        grid=(db, nc),
        in_specs=[head(kv_lora), head(DH_ROPE), tok(kv_lora), tok(DH_ROPE)]
        + _page_specs(pps, layer, n_pages, pps, PAGE_ROWS, kv_lora)
        + _page_specs(pps, layer, n_pages, pps, DH_ROPE, PAGE_ROWS),
        out_specs=head(kv_lora),
        scratch_shapes=[pltpu.VMEM((nrow, 1), F32), pltpu.VMEM((nrow, 1), F32), pltpu.VMEM((nrow, kv_lora), F32)],
    )
    return pl.pallas_call(
        functools.partial(_mla_sample_body, pps, nc, dt),
        grid_spec=grid_spec,
        out_shape=jax.ShapeDtypeStruct(ql.shape, F32),
        compiler_params=_params(2),
        name="mla_sample_attention",
    )(page_table.reshape(-1), ql, qr, lat_new, kr_new, *([cache_lat] * pps), *([cache_krt] * pps))


def _odd_out_body(til, ol_ref, ag_ref, st_ref, x_ref, gate_ref, ng_ref, wuv_ref, woc_ref, wod_ref,
                  cw_ref, cb_ref, lg_ref, lb_ref, xo_ref, xp_ref, u_ref):
    g, r = til.groups, til.rows
    tm = g * r
    kv_lora = ol_ref.shape[-1]
    d_d = ag_ref.shape[-1]
    o_all = jnp.concatenate([ol_ref[:, hd].reshape(tm, kv_lora) for hd in range(N_HEADS)], axis=1).astype(BF16)
    yc = _dot(o_all, wuv_ref[...])

    lo = _stage_conv_input(xp_ref, ag_ref[...], st_ref, CONV_D_HALO, til)
    if g == 1:
        for r0 in range(0, r, CONV_CHUNK_ROWS):
            u_ref[:, r0:r0 + CONV_CHUNK_ROWS, :] = _conv_taps(xp_ref, cw_ref, lo, CONV_D_HALO, slice(None), r0,
                                                               CONV_CHUNK_ROWS)
    else:
        gchunk = CONV_CHUNK_ROWS // r
        for g0 in range(0, g, gchunk):
            u_ref[g0:g0 + gchunk] = _conv_taps(xp_ref, cw_ref, lo, CONV_D_HALO, slice(g0, g0 + gchunk), 0, r)
    u = u_ref[...].reshape(tm, d_d) + cb_ref[...]
    mu = jnp.mean(u, axis=-1, keepdims=True)
    var = jnp.mean(jnp.square(u - mu), axis=-1, keepdims=True)
    ln = (u - mu) * lax.rsqrt(var + EPS) * lg_ref[...] + lb_ref[...]
    yd = ln * jax.nn.sigmoid(ln)
    y = _dot(yc.astype(BF16), woc_ref[...]) + _dot(yd.astype(BF16), wod_ref[...])
    xo_ref[...] = x_ref[...] + gate_ref[...] * _rms(y.reshape(g, r, y.shape[-1]), ng_ref[...])


def _odd_out(til, ol, ag, state, x, mod, m, ng, wuv, woc, wod, cw, cb, lg, lb):
    n, d = x.shape
    d_d = ag.shape[1]
    kv_lora = ol.shape[-1]
    x3 = x.reshape(n // til.rows, til.rows, d)
    ag3 = ag.reshape(n // til.rows, til.rows, d_d)
    hp = _round_up(CONV_D_HALO, SUBLANES)
    out = pl.pallas_call(
        functools.partial(_odd_out_body, til),
        grid=(til.n_tiles,),
        in_specs=[_head_spec(til, kv_lora), _x_spec(til, d_d), _state_spec(til, CONV_D_HALO, d_d),
                  _x_spec(til, d), _mod_spec(til, m, 2, d), _const_spec(ng), _const_spec(wuv), _const_spec(woc),
                  _const_spec(wod), _const_spec(cw), _const_spec(cb), _const_spec(lg), _const_spec(lb)],
        out_specs=_x_spec(til, d),
        out_shape=jax.ShapeDtypeStruct(x3.shape, F32),
        scratch_shapes=[pltpu.VMEM((til.groups, hp + til.rows, d_d), F32),
                        pltpu.VMEM((til.groups, til.rows, d_d), F32)],
        compiler_params=_params(1),
        name="odd_out",
    )(ol, ag3, state, x3, mod, ng, wuv, woc, wod, cw, cb, lg, lb)
    return out.reshape(n, d)


def _ffn_body(til, n_chunks, x_ref, sh_ref, sc_ref, gate_ref, ngi_ref, ngo_ref, wup_ref, cw_ref, cb_ref, wdn_ref,
              st_ref, xo_ref, so_ref, xp_ref):
    g, r = til.groups, til.rows
    tm = g * r
    x = x_ref[...]
    d = x.shape[-1]
    d_ff = cw_ref.shape[-1]
    cwid = d_ff // n_chunks
    h = _rms(x, ngi_ref[...]) * (1.0 + sc_ref[...]) + sh_ref[...]
    h2 = h.reshape(tm, d).astype(BF16)
    y = None
    for c in range(n_chunks):
        csl = slice(c * cwid, (c + 1) * cwid)
        u = _dot(h2, wup_ref[:, csl]).reshape(g, r, cwid)
        gt = _dot(h2, wup_ref[:, d_ff + c * cwid:d_ff + (c + 1) * cwid])
        xpc = xp_ref.at[c]
        lo = _stage_conv_input(xpc, u, st_ref.at[:, :, csl], CONV_FFN_HALO, til)
        v = _conv_taps(xpc, cw_ref.at[:, csl], lo, CONV_FFN_HALO, slice(None), 0, r).reshape(tm, cwid)
        v = v + cb_ref[:, csl]
        act = (v * jax.nn.sigmoid(v) * gt).astype(BF16)
        part = _dot(act, wdn_ref[csl, :])
        y = part if y is None else y + part
        so_ref[:, :, csl] = xpc[:, lo + r:lo + r + CONV_FFN_HALO, :]
    xo_ref[...] = x + gate_ref[...] * _rms(y.reshape(g, r, d), ngo_ref[...])


def _ffn(til, n_seq, x, mod, m, ngi, ngo, wup, cw, cb, wdn, state):
    n, d = x.shape
    d_ff = cw.shape[1]
    n_chunks = 2
    x3 = x.reshape(n // til.rows, til.rows, d)
    out, st = pl.pallas_call(
        functools.partial(_ffn_body, til, n_chunks),
        grid=(til.n_tiles,),
        in_specs=[_x_spec(til, d), _mod_spec(til, m, 0, d), _mod_spec(til, m, 1, d), _mod_spec(til, m, 2, d),
                  _const_spec(ngi), _const_spec(ngo), _const_spec(wup), _const_spec(cw), _const_spec(cb),
                  _const_spec(wdn), _state_spec(til, CONV_FFN_HALO, d_ff)],
        out_specs=[_x_spec(til, d), _state_spec(til, CONV_FFN_HALO, d_ff)],
        out_shape=[jax.ShapeDtypeStruct(x3.shape, F32), jax.ShapeDtypeStruct((n_seq, CONV_FFN_HALO, d_ff), F32)],
        scratch_shapes=[pltpu.VMEM((n_chunks, til.groups, SUBLANES + til.rows, d_ff // n_chunks), F32)],
        compiler_params=_params(1),
        name="conv_ffn",
    )(x3, mod, mod, mod, ngi, ngo, wup, cw, cb, wdn, state)
    return out.reshape(n, d), st


def _rope_tables(pos):
    half = DH_ROPE // 2
    inv = ROPE_BASE ** (-jnp.arange(half, dtype=F32) / half)
    ang = pos.astype(F32)[:, None] * inv[None, :]
    cos, sin = jnp.cos(ang), jnp.sin(ang)
    n = pos.shape[0]
    pad = jnp.zeros((n, LANES - DH_ROPE), F32)
    return (jnp.tile(cos, (1, N_HEADS)), jnp.tile(sin, (1, N_HEADS)),
            jnp.concatenate([cos, cos, pad], axis=1), jnp.concatenate([sin, sin, pad], axis=1))


def _prep_even(even_w_in, conv_a_w, fox_b_f, even_w_out, j):
    d_a = conv_a_w.shape[-1]
    hb = N_HEADS * DH_FOX
    n_main = 3 * d_a + 3 * hb
    w = even_w_in[j]
    d = w.shape[0]
    w_main = w[:, :n_main].astype(BF16)
    w_fl = jnp.concatenate([w[:, n_main:], jnp.zeros((d, LANES - N_HEADS), F32)], axis=1).astype(BF16)
    b_f = jnp.concatenate([fox_b_f[j], jnp.zeros((LANES - N_HEADS,), F32)])[None, :]
    wo = even_w_out[j].astype(BF16)
    return w_main, w_fl, b_f, conv_a_w[j], wo[:d_a], wo[d_a:]


def _prep_odd(odd_w_in, mla_q_norm, mla_w_uq, mla_kv_norm, mla_w_ukv, odd_w_out, j):
    q_lora = mla_q_norm.shape[-1]
    kv_lora = mla_kv_norm.shape[-1]
    w = odd_w_in[j]
    d = w.shape[0]
    o_kr = q_lora + kv_lora
    o_glu = o_kr + DH_ROPE
    w_main = jnp.concatenate([w[:, :o_kr], w[:, o_glu:]], axis=1).astype(BF16)
    w_kr = jnp.concatenate([w[:, o_kr:o_glu], jnp.zeros((d, LANES - DH_ROPE), F32)], axis=1).astype(BF16)
    dq = DH_NOPE + DH_ROPE
    half = DH_ROPE // 2
    hh = jnp.arange(N_HEADS)[:, None]
    nope_cols = (hh * dq + jnp.arange(DH_NOPE)[None, :]).reshape(-1)
    r1_cols = (hh * dq + DH_NOPE + jnp.arange(half)[None, :]).reshape(-1)
    r2_cols = (hh * dq + DH_NOPE + half + jnp.arange(half)[None, :]).reshape(-1)
    wuq = mla_w_uq[j][:, jnp.concatenate([nope_cols, r1_cols, r2_cols])].astype(BF16)
    w_ukv = mla_w_ukv[j].reshape(kv_lora, N_HEADS, DH_NOPE + DH_V)
    per = LANES // DH_NOPE
    wuk = jnp.transpose(w_ukv[..., :DH_NOPE], (1, 2, 0)).reshape(N_HEADS // per, LANES, kv_lora).astype(BF16)
    wuv_h = jnp.transpose(w_ukv[..., DH_NOPE:], (1, 0, 2))
    eye = jnp.eye(N_HEADS, dtype=F32)
    wuv = (wuv_h[:, :, None, :] * eye[:, None, :, None]).reshape(N_HEADS * kv_lora, N_HEADS * DH_V).astype(BF16)
    src = jnp.arange(2 * LANES)
    sel = jnp.stack([
        ((src[:, None] == jnp.where(jnp.arange(DH_ROPE) < half, hd * half + jnp.arange(DH_ROPE),
                                    LANES + hd * half + jnp.arange(DH_ROPE) - half)[None, :]))
        for hd in range(N_HEADS)]).astype(BF16)
    wo = odd_w_out[j].astype(BF16)
    n_c = N_HEADS * DH_V
    return w_main, w_kr, mla_q_norm[j][None, :], mla_kv_norm[j][None, :], wuq, wuk, sel, wuv, wo[:n_c], wo[n_c:]


def _run_trunk(x3d, mod, til, tabs, prm, states, paged):
    n_seq, seq_len, d = x3d.shape
    x = x3d.reshape(n_seq * seq_len, d)
    conv_a_in, conv_d_in, ffn_in = states
    depth = prm["norm_g"].shape[0]
    fk, fv, flf, lat_o, kro_o, ca, cd, cf = [], [], [], [], [], [], [], []
    for i in range(depth):
        j = i // 2
        ng = prm["norm_g"][i]
        if i % 2 == 0:
            w_main, w_fl, b_f, cw, wo_a, wo_b = prm["even"][j]
            res = _even_in(til, n_seq, seq_len, x, mod, 2 * i, ng[0:1], w_main, w_fl, b_f, cw, conv_a_in[j],
                           want_lft=paged is None)
            if paged is None:
                ya, q, k, v, lf, lft, st = res
                yb = _fox_prompt(n_seq, seq_len, q, k, v, lft)
            else:
                ya, q, k, v, lf, st = res
                yb = _fox_sample(j, paged["page_table"], q, k, v, lf, paged["fox_k"], paged["fox_v"],
                                 paged["fox_logf"])
            x = _out_res(til, ya, yb, wo_a, wo_b, x, mod, 2 * i, ng[1:2])
            ca.append(st)
            fk.append(k.reshape(n_seq, seq_len, N_HEADS, DH_FOX))
            fv.append(v.reshape(n_seq, seq_len, N_HEADS, DH_FOX))
            flf.append(lf.reshape(n_seq, seq_len, N_HEADS))
        else:
            w_main, w_kr, qn, kvn, wuq, wuk, sel, wuv, wo_c, wo_d = prm["odd"][j]
            ql, qr, lat, kro, ag = _odd_in(til, n_seq, seq_len, x, mod, 2 * i, ng[0:1], w_main, w_kr, qn, kvn,
                                           wuq, wuk, sel, tabs)
            if paged is None:
                ol = _mla_prompt(n_seq, seq_len, ql, qr, lat, kro)
            else:
                ol = _mla_sample(j, paged["page_table"], ql, qr, lat, kro, paged["mla_latent"], paged["mla_krope"])
            x = _odd_out(til, ol, ag, conv_d_in[j], x, mod, 2 * i, ng[1:2], wuv, wo_c, wo_d,
                         prm["conv_d_w"][j], prm["conv_d_b"][j][None, :], prm["conv_d_ln_g"][j][None, :],
                         prm["conv_d_ln_b"][j][None, :])
            d_d = ag.shape[1]
            xp = jnp.concatenate([conv_d_in[j], ag.reshape(n_seq, seq_len, d_d)], axis=1)
            cd.append(xp[:, xp.shape[1] - CONV_D_HALO:])
            lat_o.append(lat.reshape(n_seq, seq_len, -1))
            kro_o.append(kro.reshape(n_seq, seq_len, DH_ROPE))
        x, st = _ffn(til, n_seq, x, mod, 2 * i + 1, ng[2:3], ng[3:4], prm["ffn_up"][i], prm["ffn_conv_w"][i],
                     prm["ffn_conv_b"][i][None, :], prm["ffn_down"][i], ffn_in[i])
        cf.append(st)
    return (x.reshape(n_seq, seq_len, d), jnp.stack(fk), jnp.stack(fv), jnp.stack(flf), jnp.stack(lat_o),
            jnp.stack(kro_o), jnp.stack(ca), jnp.stack(cd), jnp.stack(cf))


def kernel(x_prompt, x_sample, cache_fox_k, cache_fox_v, cache_fox_logf, cache_mla_latent, cache_mla_krope, state_conv_a, state_conv_d, state_ffn_conv, page_table, c_prompt, c_sample, ada_w, ada_b, norm_g, even_w_in, conv_a_w, fox_b_f, even_w_out, odd_w_in, mla_q_norm, mla_w_uq, mla_kv_norm, mla_w_ukv, conv_d_w, conv_d_b, conv_d_ln_g, conv_d_ln_b, odd_w_out, ffn_w_up, ffn_conv_w, ffn_conv_b, ffn_w_down):
    b, t, d = x_prompt.shape
    db, dt, _ = x_sample.shape
    depth = norm_g.shape[0]
    n_even, n_odd = (depth + 1) // 2, depth // 2
    assert dt == SUBLANES and t % PROMPT_TILE == 0 and (db * dt) % SAMPLE_TILE == 0
    assert t % FOX_TQ == 0 and t % MLA_TK == 0
    assert page_table.shape[1] % FOX_PAGES_PER_STEP == 0 and page_table.shape[1] % MLA_PAGES_PER_STEP == 0

    prm = dict(
        norm_g=norm_g,
        even=[_prep_even(even_w_in, conv_a_w, fox_b_f, even_w_out, j) for j in range(n_even)],
        odd=[_prep_odd(odd_w_in, mla_q_norm, mla_w_uq, mla_kv_norm, mla_w_ukv, odd_w_out, j) for j in range(n_odd)],
        conv_d_w=conv_d_w, conv_d_b=conv_d_b, conv_d_ln_g=conv_d_ln_g, conv_d_ln_b=conv_d_ln_b,
        ffn_up=ffn_w_up.astype(BF16), ffn_conv_w=ffn_conv_w, ffn_conv_b=ffn_conv_b, ffn_down=ffn_w_down.astype(BF16),
    )
    mod_p, mod_s = _modulation(c_prompt, c_sample, ada_w, ada_b)

    til_p = _Tiling(1, PROMPT_TILE, t // PROMPT_TILE, b * t // PROMPT_TILE)
    til_s = _Tiling(SAMPLE_TILE // dt, dt, 1, db * dt // SAMPLE_TILE)
    d_a, d_d, d_ff = conv_a_w.shape[-1], conv_d_w.shape[-1], ffn_conv_w.shape[-1]
    zero_states = (jnp.zeros((n_even, b, CONV_A_HALO, d_a), F32), jnp.zeros((n_odd, b, CONV_D_HALO, d_d), F32),
                   jnp.zeros((depth, b, CONV_FFN_HALO, d_ff), F32))
    past_len = page_table.shape[1] * PAGE_ROWS
    paged = dict(page_table=page_table, fox_k=_key_minor(cache_fox_k), fox_v=_key_minor(cache_fox_v),
                 fox_logf=_key_minor(cache_fox_logf), mla_latent=cache_mla_latent,
                 mla_krope=_key_minor(cache_mla_krope))
    out_p = _run_trunk(x_prompt, mod_p, til_p, _rope_tables(jnp.arange(t, dtype=jnp.int32)), prm, zero_states, None)
    out_s = _run_trunk(x_sample, mod_s, til_s, _rope_tables(past_len + jnp.arange(dt, dtype=jnp.int32)), prm,
                       (state_conv_a, state_conv_d, state_ffn_conv), paged)
    return (out_p[0], out_s[0]) + tuple(out_p[1:]) + tuple(out_s[1:])
```

```python
import functools
from typing import NamedTuple

import jax
import jax.numpy as jnp
from jax import lax
from jax.experimental import pallas as pl
from jax.experimental.pallas import tpu as pltpu

F32 = jnp.float32
BF16 = jnp.bfloat16

EPS = 1e-6
PAGE_ROWS = 128
N_HEADS = 8
DH_FOX = 64
DH_NOPE = 64
DH_ROPE = 32
DH_V = 64
ROPE_BASE = 10000.0
CONV_A_HALO = 2
CONV_D_HALO = 30
CONV_FFN_HALO = 2
LANES = 128
SUBLANES = 8
NEG_BIG = -1e30
VMEM_LIMIT = 56 * 1024 * 1024
PROMPT_TILE = 512
SAMPLE_TILE = 512
FOX_TQ = 256
MLA_TQ = 128
MLA_TK = 512
FOX_PAGES_PER_STEP = 16
MLA_PAGE_GROUP = 4
CONV_CHUNK_ROWS = 64


class _Tiling(NamedTuple):
    groups: int
    rows: int
    tiles_per_seq: int
    n_tiles: int


def _round_up(x, m):
    return (x + m - 1) // m * m


def _dot(a, b):
    return jnp.dot(a, b, preferred_element_type=F32)


def _dot_nt(a, b):
    return lax.dot_general(a, b, (((1,), (1,)), ((), ())), preferred_element_type=F32)


def _rms(x, g):
    return x * lax.rsqrt(jnp.mean(x * x, axis=-1, keepdims=True) + EPS) * g


def _params(n_axes=1):
    return pltpu.CompilerParams(dimension_semantics=("arbitrary",) * n_axes,
                                vmem_limit_bytes=VMEM_LIMIT)


def _const_spec(arr):
    nd = arr.ndim
    return pl.BlockSpec(arr.shape, lambda *_: (0,) * nd, pipeline_mode=pl.Buffered(1))


def _mod_spec(til, m, comp, d):
    return pl.BlockSpec((None, til.groups, 1, d), lambda i: (m, i // til.tiles_per_seq, 0, comp))


def _x_spec(til, c):
    return pl.BlockSpec((til.groups, til.rows, c), lambda i: (i, 0, 0))


def _row_spec(til, c):
    return pl.BlockSpec((til.groups * til.rows, c), lambda i: (i, 0))


def _state_spec(til, halo, c):
    return pl.BlockSpec((til.groups, halo, c), lambda i: (i // til.tiles_per_seq, 0, 0))


def _head_spec(til, c):
    return pl.BlockSpec((til.groups, N_HEADS, til.rows, c),
                        lambda i: (i // til.tiles_per_seq, 0, i % til.tiles_per_seq, 0))


def _tab_spec(til):
    return pl.BlockSpec((til.rows, LANES), lambda i: (i % til.tiles_per_seq, 0))


def _stage_conv_input(xp_ref, new, st_ref, halo, til):
    rows = til.rows
    hp = _round_up(halo, SUBLANES)
    lo = hp - halo
    if til.tiles_per_seq > 1:
        first = (pl.program_id(0) % til.tiles_per_seq) == 0

        @pl.when(first)
        def _():
            xp_ref[:, lo:hp, :] = st_ref[...]

        @pl.when(jnp.logical_not(first))
        def _():
            xp_ref[:, lo:hp, :] = xp_ref[:, lo + rows:hp + rows, :]
    else:
        xp_ref[:, lo:hp, :] = st_ref[...]
    xp_ref[:, hp:hp + rows, :] = new
    return lo


def _conv_taps(xp_ref, w_ref, lo, halo, gsl, r0, nrows):
    acc = w_ref[0:1, :] * xp_ref[gsl, lo + r0:lo + r0 + nrows, :]
    for k in range(1, halo + 1):
        acc = acc + w_ref[k:k + 1, :] * xp_ref[gsl, lo + r0 + k:lo + r0 + k + nrows, :]
    return acc


def _mod_body(cp_ref, cs_ref, w_ref, b_ref, op_ref, os_ref):
    w = w_ref[...].astype(BF16)
    b = b_ref[...]
    for c_ref, o_ref in ((cp_ref, op_ref), (cs_ref, os_ref)):
        c = c_ref[...]
        a = (c * jax.nn.sigmoid(c)).astype(BF16)
        o_ref[...] = _dot(a, w) + b


def _modulation(c_p, c_s, ada_w, ada_b):
    d = c_p.shape[1]
    n_mod = ada_w.shape[0] * ada_w.shape[1]
    w = ada_w.reshape(n_mod, d, 3 * d)
    b = ada_b.reshape(n_mod, 1, 3 * d)
    bp, bs = c_p.shape[0], c_s.shape[0]
    out_p, out_s = pl.pallas_call(
        _mod_body,
        grid=(n_mod, 3),
        in_specs=[pl.BlockSpec((bp, d), lambda m, n: (0, 0)),
                  pl.BlockSpec((bs, d), lambda m, n: (0, 0)),
                  pl.BlockSpec((None, d, d), lambda m, n: (m, 0, n)),
                  pl.BlockSpec((None, 1, d), lambda m, n: (m, 0, n))],
        out_specs=[pl.BlockSpec((None, bp, d), lambda m, n: (m, 0, n)),
                   pl.BlockSpec((None, bs, d), lambda m, n: (m, 0, n))],
        out_shape=[jax.ShapeDtypeStruct((n_mod, bp, 3 * d), F32),
                   jax.ShapeDtypeStruct((n_mod, bs, 3 * d), F32)],
        compiler_params=_params(2),
        name="adaln_modulation",
    )(c_p, c_s, w, b)
    return out_p.reshape(n_mod, bp, 1, 3 * d), out_s.reshape(n_mod, bs, 1, 3 * d)


def _even_in_body(til, want_lft, x_ref, sh_ref, sc_ref, ng_ref, w_ref, wfl_ref, bf_ref, cw_ref, st_ref,
                  ya_ref, q_ref, k_ref, v_ref, lf_ref, *rest):
    if want_lft:
        lft_ref, so_ref, xp_ref = rest
    else:
        so_ref, xp_ref = rest
    g, r = til.groups, til.rows
    tm = g * r
    x = x_ref[...]
    d = x.shape[-1]
    h = _rms(x, ng_ref[...]) * (1.0 + sc_ref[...]) + sh_ref[...]
    h2 = h.reshape(tm, d).astype(BF16)
    da = ya_ref.shape[-1]

    def proj(c):
        return _dot(h2, w_ref[:, c * da:(c + 1) * da])

    xa, gb, gc = proj(0), proj(1), proj(2)
    q_ref[...] = proj(3)
    k_ref[...] = proj(4)
    v_ref[...] = proj(5)
    lf = jax.nn.log_sigmoid(_dot(h2, wfl_ref[...]) + bf_ref[...])
    lf_ref[...] = lf[:, :N_HEADS]
    if want_lft:
        lft_ref[...] = lf.T[:N_HEADS, :]
    p = (gc * xa).reshape(g, r, da)
    lo = _stage_conv_input(xp_ref, p, st_ref, CONV_A_HALO, til)
    u = _conv_taps(xp_ref, cw_ref, lo, CONV_A_HALO, slice(None), 0, r)
    ya_ref[...] = gb * u.reshape(tm, da)
    so_ref[...] = xp_ref[:, lo + r:lo + r + CONV_A_HALO, :]


def _even_in(til, n_seq, seq_len, x, mod, m, ng, w_main, w_fl, b_f, conv_w, state, want_lft):
    n, d = x.shape
    da = conv_w.shape[1]
    tm = til.groups * til.rows
    x3 = x.reshape(n // til.rows, til.rows, d)
    out_shape = [jax.ShapeDtypeStruct((n, da), F32)] * 4 + [jax.ShapeDtypeStruct((n, N_HEADS), F32)]
    out_specs = [_row_spec(til, da)] * 4 + [_row_spec(til, N_HEADS)]
    if want_lft:
        out_shape.append(jax.ShapeDtypeStruct((n_seq, N_HEADS, seq_len), F32))
        out_specs.append(pl.BlockSpec((None, N_HEADS, til.rows),
                                      lambda i: (i // til.tiles_per_seq, 0, i % til.tiles_per_seq)))
    out_shape.append(jax.ShapeDtypeStruct((n_seq, CONV_A_HALO, da), F32))
    out_specs.append(_state_spec(til, CONV_A_HALO, da))
    return pl.pallas_call(
        functools.partial(_even_in_body, til, want_lft),
        grid=(til.n_tiles,),
        in_specs=[_x_spec(til, d), _mod_spec(til, m, 0, d), _mod_spec(til, m, 1, d), _const_spec(ng),
                  _const_spec(w_main), _const_spec(w_fl), _const_spec(b_f), _const_spec(conv_w),
                  _state_spec(til, CONV_A_HALO, da)],
        out_specs=out_specs,
        out_shape=out_shape,
        scratch_shapes=[pltpu.VMEM((til.groups, SUBLANES + til.rows, da), F32)],
        compiler_params=_params(1),
        name="even_in",
    )(x3, mod, mod, ng, w_main, w_fl, b_f, conv_w, state)


def _out_res_body(til, a1_ref, a2_ref, w1_ref, w2_ref, x_ref, gate_ref, ng_ref, xo_ref):
    g, r = til.groups, til.rows
    y = _dot(a1_ref[...].astype(BF16), w1_ref[...]) + _dot(a2_ref[...].astype(BF16), w2_ref[...])
    y3 = y.reshape(g, r, y.shape[-1])
    xo_ref[...] = x_ref[...] + gate_ref[...] * _rms(y3, ng_ref[...])


def _out_res(til, a1, a2, w1, w2, x, mod, m, ng):
    n, d = x.shape
    x3 = x.reshape(n // til.rows, til.rows, d)
    out = pl.pallas_call(
        functools.partial(_out_res_body, til),
        grid=(til.n_tiles,),
        in_specs=[_row_spec(til, a1.shape[1]), _row_spec(til, a2.shape[1]), _const_spec(w1), _const_spec(w2),
                  _x_spec(til, d), _mod_spec(til, m, 2, d), _const_spec(ng)],
        out_specs=_x_spec(til, d),
        out_shape=jax.ShapeDtypeStruct(x3.shape, F32),
        compiler_params=_params(1),
        name="even_out",
    )(a1, a2, w1, w2, x3, mod, ng)
    return out.reshape(n, d)


def _cumsum_lanes(x):
    n = x.shape[-1]
    idx = lax.broadcasted_iota(jnp.int32, x.shape, x.ndim - 1)
    s = 1
    while s < n:
        x = x + jnp.where(idx >= s, pltpu.roll(x, s, axis=x.ndim - 1), 0.0)
        s *= 2
    return x


def _softmax_step(carry, s, pv):
    m, l, acc = carry
    m_new = jnp.maximum(m, jnp.max(s, axis=1, keepdims=True))
    alpha = jnp.exp(m - m_new)
    p = jnp.exp(s - m_new)
    l = alpha * l + jnp.sum(p, axis=1, keepdims=True)
    acc = alpha * acc + pv(p.astype(BF16))
    return m_new, l, acc


def _fox_prompt_body(tq, nq, q_ref, k_ref, v_ref, lft_ref, o_ref, cum_ref):
    qi = pl.program_id(1)

    @pl.when(qi == 0)
    def _():
        c = _cumsum_lanes(lft_ref[...])
        for j in range(nq):
            cum_ref[j] = c[:, j * tq:(j + 1) * tq]

    scale = DH_FOX ** -0.5
    lane = lax.broadcasted_iota(jnp.int32, (tq, LANES), 1)
    row = lax.broadcasted_iota(jnp.int32, (tq, tq), 0)
    col = lax.broadcasted_iota(jnp.int32, (tq, tq), 1)
    per = LANES // DH_FOX
    n_cg = N_HEADS // per
    qms = []
    for cg in range(n_cg):
        qc = q_ref[:, cg * LANES:(cg + 1) * LANES] * scale
        for e in range(per):
            qms.append(jnp.where(lane // DH_FOX == e, qc, 0.0).astype(BF16))

    def block(j, carry, masked):
        start = pl.multiple_of(j * tq, tq)
        cum_j = cum_ref[j]
        new = []
        for cg in range(n_cg):
            csl = slice(cg * LANES, (cg + 1) * LANES)
            kb = k_ref[pl.ds(start, tq), csl].astype(BF16)
            vb = v_ref[pl.ds(start, tq), csl].astype(BF16)
            for e in range(per):
                h = cg * per + e
                s = _dot_nt(qms[h], kb) - cum_j[h:h + 1, :]
                if masked:
                    s = jnp.where(col <= row, s, NEG_BIG)
                new.append(_softmax_step(carry[h], s, lambda p, vb=vb: _dot(p, vb)))
        return tuple(new)

    init = tuple((jnp.full((tq, 1), NEG_BIG, F32), jnp.zeros((tq, 1), F32), jnp.zeros((tq, LANES), F32))
                 for _ in range(N_HEADS))
    carry = lax.fori_loop(0, qi, lambda j, c: block(j, c, False), init)
    carry = block(qi, carry, True)
    for cg in range(n_cg):
        outs = [carry[cg * per + e][2] / carry[cg * per + e][1] for e in range(per)]
        o_ref[:, cg * LANES:(cg + 1) * LANES] = jnp.where(lane < DH_FOX, outs[0], outs[1])


def _fox_prompt(n_seq, seq_len, q, k, v, lft):
    n, c = q.shape
    tq = FOX_TQ
    nq = seq_len // tq
    return pl.pallas_call(
        functools.partial(_fox_prompt_body, tq, nq),
        grid=(n_seq, nq),
        in_specs=[pl.BlockSpec((tq, c), lambda b, i: (b * nq + i, 0)),
                  pl.BlockSpec((seq_len, c), lambda b, i: (b, 0)),
                  pl.BlockSpec((seq_len, c), lambda b, i: (b, 0)),
                  pl.BlockSpec((None, N_HEADS, seq_len), lambda b, i: (b, 0, 0))],
        out_specs=pl.BlockSpec((tq, c), lambda b, i: (b * nq + i, 0)),
        out_shape=jax.ShapeDtypeStruct((n, c), F32),
        scratch_shapes=[pltpu.VMEM((nq, N_HEADS, tq), F32)],
        compiler_params=_params(2),
        name="fox_prompt_attention",
    )(q, k, v, lft)


def _rows_to_lanes(x, pad_ref):
    pad_ref[...] = jnp.zeros(pad_ref.shape, F32)
    pad_ref[0:x.shape[0], 0:x.shape[1]] = x
    return pad_ref[...].T[0:x.shape[1], :]


def _pad_rows(x, rows):
    return jnp.concatenate([x, jnp.zeros((rows - x.shape[0], x.shape[1]), x.dtype)], axis=0)


def _split3(x):
    hi = x.astype(BF16)
    r1 = x - hi.astype(F32)
    mid = r1.astype(BF16)
    lo = (r1 - mid.astype(F32)).astype(BF16)
    return hi, mid, lo


def _dot_f32_by_01(x, w01):
    hi, mid, lo = _split3(x)
    return _dot(hi, w01) + _dot(mid, w01) + _dot(lo, w01)


def _fox_sample_body(pps, nc, dt, pt_ref, q_ref, kn_ref, vn_ref, lfn_ref, tri_ref, *rest):
    k_refs = rest[:pps]
    v_refs = rest[pps:2 * pps]
    lf_refs = rest[2 * pps:3 * pps]
    o_ref, m_ref, l_ref, acc_ref, pre_ref, pad_ref = rest[3 * pps:]
    c = pl.program_id(1)
    nrow = N_HEADS * dt
    width = N_HEADS * DH_FOX

    @pl.when(c == 0)
    def _():
        m_ref[...] = jnp.full(m_ref.shape, NEG_BIG, F32)
        l_ref[...] = jnp.zeros(l_ref.shape, F32)
        acc_ref[...] = jnp.zeros(acc_ref.shape, F32)
        pre_ref[...] = jnp.zeros(pre_ref.shape, F32)

    q = q_ref[...] * (DH_FOX ** -0.5)
    qt = jnp.broadcast_to(q[None], (N_HEADS, dt, width)).reshape(nrow, width)
    rh = lax.broadcasted_iota(jnp.int32, (nrow, width), 0) // dt
    ch = lax.broadcasted_iota(jnp.int32, (nrow, width), 1) // DH_FOX
    qbd = jnp.where(rh == ch, qt, 0.0).astype(BF16)

    def expand_heads(x):
        return jnp.broadcast_to(x[:, None, :], (N_HEADS, dt, x.shape[-1])).reshape(nrow, x.shape[-1])

    def update(s, pv):
        carry = _softmax_step((m_ref[...], l_ref[...], acc_ref[...]), s, pv)
        m_ref[...], l_ref[...], acc_ref[...] = carry

    x_all = jnp.concatenate([r[...] for r in lf_refs], axis=0)
    cw = _dot_f32_by_01(x_all, tri_ref[...])
    pre = pre_ref[...]
    cols = []
    for i in range(pps):
        rsl = slice(i * N_HEADS, (i + 1) * N_HEADS)
        cum_i = cw[rsl, 0:PAGE_ROWS] + pre
        pre = pre + cw[rsl, PAGE_ROWS:2 * PAGE_ROWS]
        cols.append(_dot(qbd, k_refs[i][...].astype(BF16)) - expand_heads(cum_i))
    pre_ref[...] = pre
    s = jnp.concatenate(cols, axis=1)

    def pv_past(p):
        out = _dot_nt(p[:, 0:PAGE_ROWS], v_refs[0][...].astype(BF16))
        for i in range(1, pps):
            out = out + _dot_nt(p[:, i * PAGE_ROWS:(i + 1) * PAGE_ROWS], v_refs[i][...].astype(BF16))
        return out

    update(s, pv_past)

    @pl.when(c == nc - 1)
    def _():
        lfn = _rows_to_lanes(lfn_ref[...], pad_ref)
        cumn = _dot_f32_by_01(lfn, tri_ref[:, 0:PAGE_ROWS]) + pre_ref[...]
        kn = _pad_rows(kn_ref[...], PAGE_ROWS).astype(BF16)
        vn = _pad_rows(vn_ref[...], PAGE_ROWS).astype(BF16)
        sn = _dot_nt(qbd, kn) - expand_heads(cumn)
        t = lax.broadcasted_iota(jnp.int32, (nrow, PAGE_ROWS), 0) % dt
        key = lax.broadcasted_iota(jnp.int32, (nrow, PAGE_ROWS), 1)
        sn = jnp.where(key <= t, sn, NEG_BIG)
        update(sn, lambda p: _dot(p, vn))
        o = (acc_ref[...] / l_ref[...]).reshape(N_HEADS, dt, width)
        hh = lax.broadcasted_iota(jnp.int32, (N_HEADS, dt, width), 0)
        cc = lax.broadcasted_iota(jnp.int32, (N_HEADS, dt, width), 2) // DH_FOX
        o_ref[...] = jnp.sum(jnp.where(hh == cc, o, 0.0), axis=0)


def _page_specs(n, layer, n_pages, pps, rows, c):
    return [pl.BlockSpec((None, None, rows, c),
                         lambda b, ch, pt, i=i: (layer, pt[b * n_pages + ch * pps + i], 0, 0))
            for i in range(n)]


def _key_minor(cache):
    nd = cache.ndim
    moved = jnp.transpose(cache, (0, 1) + tuple(range(3, nd)) + (2,))
    return moved.reshape(cache.shape[0], cache.shape[1], -1, cache.shape[2])


def _fox_sample(layer, page_table, q, k_new, v_new, lf_new, cache_kt, cache_vt, cache_lft):
    db, n_pages = page_table.shape
    n, width = q.shape
    dt = n // db
    pps = FOX_PAGES_PER_STEP
    nc = n_pages // pps
    nrow = N_HEADS * dt
    tri = jnp.concatenate([jnp.triu(jnp.ones((PAGE_ROWS, PAGE_ROWS), F32)), jnp.ones((PAGE_ROWS, PAGE_ROWS), F32)],
                          axis=1).astype(BF16)
    tok = lambda cdim: pl.BlockSpec((dt, cdim), lambda b, ch, pt: (b, 0))
    grid_spec = pltpu.PrefetchScalarGridSpec(
        num_scalar_prefetch=1,
        grid=(db, nc),
        in_specs=[tok(width), tok(width), tok(width), tok(N_HEADS),
                  pl.BlockSpec(tri.shape, lambda b, ch, pt: (0, 0))]
        + _page_specs(pps, layer, n_pages, pps, width, PAGE_ROWS)
        + _page_specs(pps, layer, n_pages, pps, width, PAGE_ROWS)
        + _page_specs(pps, layer, n_pages, pps, N_HEADS, PAGE_ROWS),
        out_specs=tok(width),
        scratch_shapes=[pltpu.VMEM((nrow, 1), F32), pltpu.VMEM((nrow, 1), F32), pltpu.VMEM((nrow, width), F32),
                        pltpu.VMEM((N_HEADS, LANES), F32), pltpu.VMEM((PAGE_ROWS, LANES), F32)],
    )
    return pl.pallas_call(
        functools.partial(_fox_sample_body, pps, nc, dt),
        grid_spec=grid_spec,
        out_shape=jax.ShapeDtypeStruct((n, width), F32),
        compiler_params=_params(2),
        name="fox_sample_attention",
    )(page_table.reshape(-1), q, k_new, v_new, lf_new, tri, *([cache_kt] * pps), *([cache_vt] * pps),
      *([cache_lft] * pps))


def _odd_in_body(til, x_ref, sh_ref, sc_ref, ng_ref, w_ref, wkr_ref, qn_ref, kvn_ref, wuq_ref, wuk_ref, sel_ref,
                 cos8_ref, sin8_ref, cosk_ref, sink_ref, ql_ref, qr_ref, lat_ref, kro_ref, ag_ref):
    g, r = til.groups, til.rows
    tm = g * r
    x = x_ref[...]
    d = x.shape[-1]
    h = _rms(x, ng_ref[...]) * (1.0 + sc_ref[...]) + sh_ref[...]
    h2 = h.reshape(tm, d).astype(BF16)
    q_lora = qn_ref.shape[-1]
    kv_lora = kvn_ref.shape[-1]
    d_d = ag_ref.shape[-1]
    o_kv = q_lora
    o_a = o_kv + kv_lora
    o_g = o_a + d_d

    def bcast_tab(ref):
        return jnp.broadcast_to(ref[...][None], (g, r, LANES)).reshape(tm, LANES)

    cq = _rms(_dot(h2, w_ref[:, 0:o_kv]), qn_ref[...]).astype(BF16)
    qq = _dot(cq, wuq_ref[...])
    n_nope = N_HEADS * DH_NOPE
    r1 = qq[:, n_nope:n_nope + LANES]
    r2 = qq[:, n_nope + LANES:n_nope + 2 * LANES]
    cos8, sin8 = bcast_tab(cos8_ref), bcast_tab(sin8_ref)
    qr_all = jnp.concatenate([r1 * cos8 - r2 * sin8, r1 * sin8 + r2 * cos8], axis=1).astype(BF16)
    lane = lax.broadcasted_iota(jnp.int32, (tm, LANES), 1)
    per = LANES // DH_NOPE
    for hd in range(N_HEADS):
        cg, e = hd // per, hd % per
        qn = jnp.where(lane // DH_NOPE == e, qq[:, cg * LANES:(cg + 1) * LANES], 0.0).astype(BF16)
        ql_ref[:, hd] = _dot(qn, wuk_ref[cg]).reshape(g, r, kv_lora)
        qr_ref[:, hd] = _dot(qr_all, sel_ref[hd]).reshape(g, r, DH_ROPE)

    lat_ref[...] = _rms(_dot(h2, w_ref[:, o_kv:o_a]), kvn_ref[...])
    kr = _dot(h2, wkr_ref[...])
    half = DH_ROPE // 2
    swapped = jnp.where(lane < half, -pltpu.roll(kr, LANES - half, axis=1), pltpu.roll(kr, half, axis=1))
    kro = kr * bcast_tab(cosk_ref) + swapped * bcast_tab(sink_ref)
    kro_ref[...] = kro[:, 0:DH_ROPE]

    a = _dot(h2, w_ref[:, o_a:o_g])
    gt = _dot(h2, w_ref[:, o_g:o_g + d_d])
    ag_ref[...] = a * jax.nn.sigmoid(gt)


def _odd_in(til, n_seq, seq_len, x, mod, m, ng, w_main, w_kr, qn, kvn, wuq, wuk, sel, tabs):
    n, d = x.shape
    x3 = x.reshape(n // til.rows, til.rows, d)
    kv_lora = kvn.shape[-1]
    d_d = (w_main.shape[1] - qn.shape[-1] - kv_lora) // 2
    cos8, sin8, cosk, sink = tabs
    return pl.pallas_call(
        functools.partial(_odd_in_body, til),
        grid=(til.n_tiles,),
        in_specs=[_x_spec(til, d), _mod_spec(til, m, 0, d), _mod_spec(til, m, 1, d), _const_spec(ng),
                  _const_spec(w_main), _const_spec(w_kr), _const_spec(qn), _const_spec(kvn), _const_spec(wuq),
                  _const_spec(wuk), _const_spec(sel),
                  _tab_spec(til), _tab_spec(til), _tab_spec(til), _tab_spec(til)],
        out_specs=[_head_spec(til, kv_lora), _head_spec(til, DH_ROPE), _row_spec(til, kv_lora),
                   _row_spec(til, DH_ROPE), _row_spec(til, d_d)],
        out_shape=[jax.ShapeDtypeStruct((n_seq, N_HEADS, seq_len, kv_lora), F32),
                   jax.ShapeDtypeStruct((n_seq, N_HEADS, seq_len, DH_ROPE), F32),
                   jax.ShapeDtypeStruct((n, kv_lora), F32),
                   jax.ShapeDtypeStruct((n, DH_ROPE), F32),
                   jax.ShapeDtypeStruct((n, d_d), F32)],
        compiler_params=_params(1),
        name="odd_in",
    )(x3, mod, mod, ng, w_main, w_kr, qn, kvn, wuq, wuk, sel, cos8, sin8, cosk, sink)


def _mla_prompt_body(tq, tk, ql_ref, qr_ref, lat_ref, kr_ref, o_ref):
    qi = pl.program_id(1)
    m_rows = N_HEADS * tq
    kv_lora = ql_ref.shape[-1]
    ql = ql_ref[...].reshape(m_rows, kv_lora).astype(BF16)
    qr = qr_ref[...].reshape(m_rows, DH_ROPE).astype(BF16)
    scale = (DH_NOPE + DH_ROPE) ** -0.5
    t_pos = qi * tq + lax.broadcasted_iota(jnp.int32, (m_rows, tk), 0) % tq
    col = lax.broadcasted_iota(jnp.int32, (m_rows, tk), 1)

    def step(j, carry, masked):
        start = pl.multiple_of(j * tk, tk)
        lat = lat_ref[pl.ds(start, tk), :].astype(BF16)
        kr = kr_ref[pl.ds(start, tk), :].astype(BF16)
        s = (_dot_nt(ql, lat) + _dot_nt(qr, kr)) * scale
        if masked:
            s = jnp.where(j * tk + col <= t_pos, s, NEG_BIG)
        return _softmax_step(carry, s, lambda p: _dot(p, lat))

    init = (jnp.full((m_rows, 1), NEG_BIG, F32), jnp.zeros((m_rows, 1), F32), jnp.zeros((m_rows, kv_lora), F32))
    n_full = (qi * tq) // tk
    carry = lax.fori_loop(0, n_full, lambda j, c: step(j, c, False), init)
    _, l, acc = step(n_full, carry, True)
    o_ref[...] = (acc / l).reshape(N_HEADS, tq, kv_lora)


def _mla_prompt(n_seq, seq_len, ql, qr, lat, kr):
    kv_lora = lat.shape[1]
    tq, tk = MLA_TQ, MLA_TK
    nq = seq_len // tq
    return pl.pallas_call(
        functools.partial(_mla_prompt_body, tq, tk),
        grid=(n_seq, nq),
        in_specs=[pl.BlockSpec((None, N_HEADS, tq, kv_lora), lambda b, i: (b, 0, i, 0)),
                  pl.BlockSpec((None, N_HEADS, tq, DH_ROPE), lambda b, i: (b, 0, i, 0)),
                  pl.BlockSpec((seq_len, kv_lora), lambda b, i: (b, 0)),
                  pl.BlockSpec((seq_len, DH_ROPE), lambda b, i: (b, 0))],
        out_specs=pl.BlockSpec((None, N_HEADS, tq, kv_lora), lambda b, i: (b, 0, i, 0)),
        out_shape=jax.ShapeDtypeStruct(ql.shape, F32),
        compiler_params=_params(2),
        name="mla_prompt_attention",
    )(ql, qr, lat, kr)


def _mla_sample_body(layer, n_pages, dt, pt_ref, ql_ref, qr_ref, latn_ref, krn_ref, lat_hbm, krt_hbm, o_ref,
                     lat_buf, kr_buf, sem):
    b = pl.program_id(0)
    slot = b % 2
    nrow = N_HEADS * dt
    kv_lora = ql_ref.shape[-1]
    scale = (DH_NOPE + DH_ROPE) ** -0.5

    def page_copies(seq, sl, i):
        page = pt_ref[seq * n_pages + i]
        return (pltpu.make_async_copy(lat_hbm.at[layer, page], lat_buf.at[sl, i], sem.at[sl, 0]),
                pltpu.make_async_copy(krt_hbm.at[layer, page], kr_buf.at[sl, i], sem.at[sl, 1]))

    def start_all(seq, sl):
        def body(i, carry):
            for cp in page_copies(seq, sl, i):
                cp.start()
            return carry
        lax.fori_loop(0, n_pages, body, 0)

    @pl.when(b == 0)
    def _():
        start_all(0, 0)

    @pl.when(b + 1 < pl.num_programs(0))
    def _():
        start_all(b + 1, 1 - slot)

    def wait_body(i, carry):
        for cp in page_copies(b, slot, i):
            cp.wait()
        return carry
    lax.fori_loop(0, n_pages, wait_body, 0)

    ql = ql_ref[...].reshape(nrow, kv_lora).astype(BF16)
    qr = qr_ref[...].reshape(nrow, DH_ROPE).astype(BF16)
    grp = MLA_PAGE_GROUP
    gw = grp * PAGE_ROWS
    n_grp = n_pages // grp
    lats, cols = [], []
    for g in range(n_grp):
        lat_g = lat_buf[slot, g * grp:(g + 1) * grp].reshape(gw, kv_lora).astype(BF16)
        kr_g = jnp.concatenate([kr_buf[slot, g * grp + t] for t in range(grp)], axis=1).astype(BF16)
        lats.append(lat_g)
        cols.append(_dot_nt(ql, lat_g) + _dot(qr, kr_g))
    s = jnp.concatenate(cols, axis=1) * scale

    latn = _pad_rows(latn_ref[...], PAGE_ROWS).astype(BF16)
    krn = _pad_rows(krn_ref[...], PAGE_ROWS).astype(BF16)
    sn = (_dot_nt(ql, latn) + _dot_nt(qr, krn)) * scale
    t = lax.broadcasted_iota(jnp.int32, (nrow, PAGE_ROWS), 0) % dt
    key = lax.broadcasted_iota(jnp.int32, (nrow, PAGE_ROWS), 1)
    sn = jnp.where(key <= t, sn, NEG_BIG)

    m = jnp.maximum(jnp.max(s, axis=1, keepdims=True), jnp.max(sn, axis=1, keepdims=True))
    p = jnp.exp(s - m)
    pn = jnp.exp(sn - m)
    l = jnp.sum(p, axis=1, keepdims=True) + jnp.sum(pn, axis=1, keepdims=True)
    pb = p.astype(BF16)
    acc = _dot(pn.astype(BF16), latn)
    for g in range(n_grp):
        acc = acc + _dot(pb[:, g * gw:(g + 1) * gw], lats[g])
    o_ref[...] = (acc / l).reshape(N_HEADS, dt, kv_lora)


def _mla_sample(layer, page_table, ql, qr, lat_new, kr_new, cache_lat, cache_krt):
    db, n_pages = page_table.shape
    dt = ql.shape[2]
    kv_lora = ql.shape[3]
    assert n_pages % MLA_PAGE_GROUP == 0
    head = lambda cdim: pl.BlockSpec((None, N_HEADS, dt, cdim), lambda b, pt: (b, 0, 0, 0))
    tok = lambda cdim: pl.BlockSpec((dt, cdim), lambda b, pt: (b, 0))
    grid_spec = pltpu.PrefetchScalarGridSpec(
        num_scalar_prefetch=1,
        grid=(db,),
        in_specs=[head(kv_lora), head(DH_ROPE), tok(kv_lora), tok(DH_ROPE),
                  pl.BlockSpec(memory_space=pl.ANY), pl.BlockSpec(memory_space=pl.ANY)],
        out_specs=head(kv_lora),
        scratch_shapes=[pltpu.VMEM((2, n_pages, PAGE_ROWS, kv_lora), F32),
                        pltpu.VMEM((2, n_pages, DH_ROPE, PAGE_ROWS), F32),
                        pltpu.SemaphoreType.DMA((2, 2))],
    )
    return pl.pallas_call(
        functools.partial(_mla_sample_body, layer, n_pages, dt),
        grid_spec=grid_spec,
        out_shape=jax.ShapeDtypeStruct(ql.shape, F32),
        compiler_params=_params(1),
        name="mla_sample_attention",
    )(page_table.reshape(-1), ql, qr, lat_new, kr_new, cache_lat, cache_krt)


def _odd_out_body(til, ol_ref, ag_ref, st_ref, x_ref, gate_ref, ng_ref, wuv_ref, woc_ref, wod_ref,
                  cw_ref, cb_ref, lg_ref, lb_ref, xo_ref, xp_ref, u_ref):
    g, r = til.groups, til.rows
    tm = g * r
    kv_lora = ol_ref.shape[-1]
    d_d = ag_ref.shape[-1]
    o_all = jnp.concatenate([ol_ref[:, hd].reshape(tm, kv_lora) for hd in range(N_HEADS)], axis=1).astype(BF16)
    yc = _dot(o_all, wuv_ref[...])

    lo = _stage_conv_input(xp_ref, ag_ref[...], st_ref, CONV_D_HALO, til)
    if g == 1:
        for r0 in range(0, r, CONV_CHUNK_ROWS):
            u_ref[:, r0:r0 + CONV_CHUNK_ROWS, :] = _conv_taps(xp_ref, cw_ref, lo, CONV_D_HALO, slice(None), r0,
                                                               CONV_CHUNK_ROWS)
    else:
        gchunk = CONV_CHUNK_ROWS // r
        for g0 in range(0, g, gchunk):
            u_ref[g0:g0 + gchunk] = _conv_taps(xp_ref, cw_ref, lo, CONV_D_HALO, slice(g0, g0 + gchunk), 0, r)
    u = u_ref[...].reshape(tm, d_d) + cb_ref[...]
    mu = jnp.mean(u, axis=-1, keepdims=True)
    var = jnp.mean(jnp.square(u - mu), axis=-1, keepdims=True)
    ln = (u - mu) * lax.rsqrt(var + EPS) * lg_ref[...] + lb_ref[...]
    yd = ln * jax.nn.sigmoid(ln)
    y = _dot(yc.astype(BF16), woc_ref[...]) + _dot(yd.astype(BF16), wod_ref[...])
    xo_ref[...] = x_ref[...] + gate_ref[...] * _rms(y.reshape(g, r, y.shape[-1]), ng_ref[...])


def _odd_out(til, ol, ag, state, x, mod, m, ng, wuv, woc, wod, cw, cb, lg, lb):
    n, d = x.shape
    d_d = ag.shape[1]
    kv_lora = ol.shape[-1]
    x3 = x.reshape(n // til.rows, til.rows, d)
    ag3 = ag.reshape(n // til.rows, til.rows, d_d)
    hp = _round_up(CONV_D_HALO, SUBLANES)
    out = pl.pallas_call(
        functools.partial(_odd_out_body, til),
        grid=(til.n_tiles,),
        in_specs=[_head_spec(til, kv_lora), _x_spec(til, d_d), _state_spec(til, CONV_D_HALO, d_d),
                  _x_spec(til, d), _mod_spec(til, m, 2, d), _const_spec(ng), _const_spec(wuv), _const_spec(woc),
                  _const_spec(wod), _const_spec(cw), _const_spec(cb), _const_spec(lg), _const_spec(lb)],
        out_specs=_x_spec(til, d),
        out_shape=jax.ShapeDtypeStruct(x3.shape, F32),
        scratch_shapes=[pltpu.VMEM((til.groups, hp + til.rows, d_d), F32),
                        pltpu.VMEM((til.groups, til.rows, d_d), F32)],
        compiler_params=_params(1),
        name="odd_out",
    )(ol, ag3, state, x3, mod, ng, wuv, woc, wod, cw, cb, lg, lb)
    return out.reshape(n, d)


def _ffn_body(til, n_chunks, x_ref, sh_ref, sc_ref, gate_ref, ngi_ref, ngo_ref, wup_ref, cw_ref, cb_ref, wdn_ref,
              st_ref, xo_ref, so_ref, xp_ref):
    g, r = til.groups, til.rows
    tm = g * r
    x = x_ref[...]
    d = x.shape[-1]
    d_ff = cw_ref.shape[-1]
    cwid = d_ff // n_chunks
    h = _rms(x, ngi_ref[...]) * (1.0 + sc_ref[...]) + sh_ref[...]
    h2 = h.reshape(tm, d).astype(BF16)
    y = None
    for c in range(n_chunks):
        csl = slice(c * cwid, (c + 1) * cwid)
        u = _dot(h2, wup_ref[:, csl]).reshape(g, r, cwid)
        gt = _dot(h2, wup_ref[:, d_ff + c * cwid:d_ff + (c + 1) * cwid])
        xpc = xp_ref.at[c]
        lo = _stage_conv_input(xpc, u, st_ref.at[:, :, csl], CONV_FFN_HALO, til)
        v = _conv_taps(xpc, cw_ref.at[:, csl], lo, CONV_FFN_HALO, slice(None), 0, r).reshape(tm, cwid)
        v = v + cb_ref[:, csl]
        act = (v * jax.nn.sigmoid(v) * gt).astype(BF16)
        part = _dot(act, wdn_ref[csl, :])
        y = part if y is None else y + part
        so_ref[:, :, csl] = xpc[:, lo + r:lo + r + CONV_FFN_HALO, :]
    xo_ref[...] = x + gate_ref[...] * _rms(y.reshape(g, r, d), ngo_ref[...])


def _ffn(til, n_seq, x, mod, m, ngi, ngo, wup, cw, cb, wdn, state):
    n, d = x.shape
    d_ff = cw.shape[1]
    n_chunks = 2
    x3 = x.reshape(n // til.rows, til.rows, d)
    out, st = pl.pallas_call(
        functools.partial(_ffn_body, til, n_chunks),
        grid=(til.n_tiles,),
        in_specs=[_x_spec(til, d), _mod_spec(til, m, 0, d), _mod_spec(til, m, 1, d), _mod_spec(til, m, 2, d),
                  _const_spec(ngi), _const_spec(ngo), _const_spec(wup), _const_spec(cw), _const_spec(cb),
                  _const_spec(wdn), _state_spec(til, CONV_FFN_HALO, d_ff)],
        out_specs=[_x_spec(til, d), _state_spec(til, CONV_FFN_HALO, d_ff)],
        out_shape=[jax.ShapeDtypeStruct(x3.shape, F32), jax.ShapeDtypeStruct((n_seq, CONV_FFN_HALO, d_ff), F32)],
        scratch_shapes=[pltpu.VMEM((n_chunks, til.groups, SUBLANES + til.rows, d_ff // n_chunks), F32)],
        compiler_params=_params(1),
        name="conv_ffn",
    )(x3, mod, mod, mod, ngi, ngo, wup, cw, cb, wdn, state)
    return out.reshape(n, d), st


def _rope_tables(pos):
    half = DH_ROPE // 2
    inv = ROPE_BASE ** (-jnp.arange(half, dtype=F32) / half)
    ang = pos.astype(F32)[:, None] * inv[None, :]
    cos, sin = jnp.cos(ang), jnp.sin(ang)
    n = pos.shape[0]
    pad = jnp.zeros((n, LANES - DH_ROPE), F32)
    return (jnp.tile(cos, (1, N_HEADS)), jnp.tile(sin, (1, N_HEADS)),
            jnp.concatenate([cos, cos, pad], axis=1), jnp.concatenate([sin, sin, pad], axis=1))


def _prep_even(even_w_in, conv_a_w, fox_b_f, even_w_out, j):
    d_a = conv_a_w.shape[-1]
    hb = N_HEADS * DH_FOX
    n_main = 3 * d_a + 3 * hb
    w = even_w_in[j]
    d = w.shape[0]
    w_main = w[:, :n_main].astype(BF16)
    w_fl = jnp.concatenate([w[:, n_main:], jnp.zeros((d, LANES - N_HEADS), F32)], axis=1).astype(BF16)
    b_f = jnp.concatenate([fox_b_f[j], jnp.zeros((LANES - N_HEADS,), F32)])[None, :]
    wo = even_w_out[j].astype(BF16)
    return w_main, w_fl, b_f, conv_a_w[j], wo[:d_a], wo[d_a:]


def _prep_odd(odd_w_in, mla_q_norm, mla_w_uq, mla_kv_norm, mla_w_ukv, odd_w_out, j):
    q_lora = mla_q_norm.shape[-1]
    kv_lora = mla_kv_norm.shape[-1]
    w = odd_w_in[j]
    d = w.shape[0]
    o_kr = q_lora + kv_lora
    o_glu = o_kr + DH_ROPE
    w_main = jnp.concatenate([w[:, :o_kr], w[:, o_glu:]], axis=1).astype(BF16)
    w_kr = jnp.concatenate([w[:, o_kr:o_glu], jnp.zeros((d, LANES - DH_ROPE), F32)], axis=1).astype(BF16)
    dq = DH_NOPE + DH_ROPE
    half = DH_ROPE // 2
    hh = jnp.arange(N_HEADS)[:, None]
    nope_cols = (hh * dq + jnp.arange(DH_NOPE)[None, :]).reshape(-1)
    r1_cols = (hh * dq + DH_NOPE + jnp.arange(half)[None, :]).reshape(-1)
    r2_cols = (hh * dq + DH_NOPE + half + jnp.arange(half)[None, :]).reshape(-1)
    wuq = mla_w_uq[j][:, jnp.concatenate([nope_cols, r1_cols, r2_cols])].astype(BF16)
    w_ukv = mla_w_ukv[j].reshape(kv_lora, N_HEADS, DH_NOPE + DH_V)
    per = LANES // DH_NOPE
    wuk = jnp.transpose(w_ukv[..., :DH_NOPE], (1, 2, 0)).reshape(N_HEADS // per, LANES, kv_lora).astype(BF16)
    wuv_h = jnp.transpose(w_ukv[..., DH_NOPE:], (1, 0, 2))
    eye = jnp.eye(N_HEADS, dtype=F32)
    wuv = (wuv_h[:, :, None, :] * eye[:, None, :, None]).reshape(N_HEADS * kv_lora, N_HEADS * DH_V).astype(BF16)
    src = jnp.arange(2 * LANES)
    sel = jnp.stack([
        ((src[:, None] == jnp.where(jnp.arange(DH_ROPE) < half, hd * half + jnp.arange(DH_ROPE),
                                    LANES + hd * half + jnp.arange(DH_ROPE) - half)[None, :]))
        for hd in range(N_HEADS)]).astype(BF16)
    wo = odd_w_out[j].astype(BF16)
    n_c = N_HEADS * DH_V
    return w_main, w_kr, mla_q_norm[j][None, :], mla_kv_norm[j][None, :], wuq, wuk, sel, wuv, wo[:n_c], wo[n_c:]


def _run_trunk(x3d, mod, til, tabs, prm, states, paged):
    n_seq, seq_len, d = x3d.shape
    x = x3d.reshape(n_seq * seq_len, d)
    conv_a_in, conv_d_in, ffn_in = states
    depth = prm["norm_g"].shape[0]
    fk, fv, flf, lat_o, kro_o, ca, cd, cf = [], [], [], [], [], [], [], []
    for i in range(depth):
        j = i // 2
        ng = prm["norm_g"][i]
        if i % 2 == 0:
            w_main, w_fl, b_f, cw, wo_a, wo_b = prm["even"][j]
            res = _even_in(til, n_seq, seq_len, x, mod, 2 * i, ng[0:1], w_main, w_fl, b_f, cw, conv_a_in[j],
                           want_lft=paged is None)
            if paged is None:
                ya, q, k, v, lf, lft, st = res
                yb = _fox_prompt(n_seq, seq_len, q, k, v, lft)
            else:
                ya, q, k, v, lf, st = res
                yb = _fox_sample(j, paged["page_table"], q, k, v, lf, paged["fox_k"], paged["fox_v"],
                                 paged["fox_logf"])
            x = _out_res(til, ya, yb, wo_a, wo_b, x, mod, 2 * i, ng[1:2])
            ca.append(st)
            fk.append(k.reshape(n_seq, seq_len, N_HEADS, DH_FOX))
            fv.append(v.reshape(n_seq, seq_len, N_HEADS, DH_FOX))
            flf.append(lf.reshape(n_seq, seq_len, N_HEADS))
        else:
            w_main, w_kr, qn, kvn, wuq, wuk, sel, wuv, wo_c, wo_d = prm["odd"][j]
            ql, qr, lat, kro, ag = _odd_in(til, n_seq, seq_len, x, mod, 2 * i, ng[0:1], w_main, w_kr, qn, kvn,
                                           wuq, wuk, sel, tabs)
            if paged is None:
                ol = _mla_prompt(n_seq, seq_len, ql, qr, lat, kro)
            else:
                ol = _mla_sample(j, paged["page_table"], ql, qr, lat, kro, paged["mla_latent"], paged["mla_krope"])
            x = _odd_out(til, ol, ag, conv_d_in[j], x, mod, 2 * i, ng[1:2], wuv, wo_c, wo_d,
                         prm["conv_d_w"][j], prm["conv_d_b"][j][None, :], prm["conv_d_ln_g"][j][None, :],
                         prm["conv_d_ln_b"][j][None, :])
            d_d = ag.shape[1]
            xp = jnp.concatenate([conv_d_in[j], ag.reshape(n_seq, seq_len, d_d)], axis=1)
            cd.append(xp[:, xp.shape[1] - CONV_D_HALO:])
            lat_o.append(lat.reshape(n_seq, seq_len, -1))
            kro_o.append(kro.reshape(n_seq, seq_len, DH_ROPE))
        x, st = _ffn(til, n_seq, x, mod, 2 * i + 1, ng[2:3], ng[3:4], prm["ffn_up"][i], prm["ffn_conv_w"][i],
                     prm["ffn_conv_b"][i][None, :], prm["ffn_down"][i], ffn_in[i])
        cf.append(st)
    return (x.reshape(n_seq, seq_len, d), jnp.stack(fk), jnp.stack(fv), jnp.stack(flf), jnp.stack(lat_o),
            jnp.stack(kro_o), jnp.stack(ca), jnp.stack(cd), jnp.stack(cf))


def kernel(x_prompt, x_sample, cache_fox_k, cache_fox_v, cache_fox_logf, cache_mla_latent, cache_mla_krope, state_conv_a, state_conv_d, state_ffn_conv, page_table, c_prompt, c_sample, ada_w, ada_b, norm_g, even_w_in, conv_a_w, fox_b_f, even_w_out, odd_w_in, mla_q_norm, mla_w_uq, mla_kv_norm, mla_w_ukv, conv_d_w, conv_d_b, conv_d_ln_g, conv_d_ln_b, odd_w_out, ffn_w_up, ffn_conv_w, ffn_conv_b, ffn_w_down):
    b, t, d = x_prompt.shape
    db, dt, _ = x_sample.shape
    depth = norm_g.shape[0]
    n_even, n_odd = (depth + 1) // 2, depth // 2
    assert dt == SUBLANES and t % PROMPT_TILE == 0 and (db * dt) % SAMPLE_TILE == 0
    assert t % FOX_TQ == 0 and t % MLA_TK == 0
    assert page_table.shape[1] % FOX_PAGES_PER_STEP == 0

    prm = dict(
        norm_g=norm_g,
        even=[_prep_even(even_w_in, conv_a_w, fox_b_f, even_w_out, j) for j in range(n_even)],
        odd=[_prep_odd(odd_w_in, mla_q_norm, mla_w_uq, mla_kv_norm, mla_w_ukv, odd_w_out, j) for j in range(n_odd)],
        conv_d_w=conv_d_w, conv_d_b=conv_d_b, conv_d_ln_g=conv_d_ln_g, conv_d_ln_b=conv_d_ln_b,
        ffn_up=ffn_w_up.astype(BF16), ffn_conv_w=ffn_conv_w, ffn_conv_b=ffn_conv_b, ffn_down=ffn_w_down.astype(BF16),
    )
    mod_p, mod_s = _modulation(c_prompt, c_sample, ada_w, ada_b)

    til_p = _Tiling(1, PROMPT_TILE, t // PROMPT_TILE, b * t // PROMPT_TILE)
    til_s = _Tiling(SAMPLE_TILE // dt, dt, 1, db * dt // SAMPLE_TILE)
    d_a, d_d, d_ff = conv_a_w.shape[-1], conv_d_w.shape[-1], ffn_conv_w.shape[-1]
    zero_states = (jnp.zeros((n_even, b, CONV_A_HALO, d_a), F32), jnp.zeros((n_odd, b, CONV_D_HALO, d_d), F32),
                   jnp.zeros((depth, b, CONV_FFN_HALO, d_ff), F32))
    past_len = page_table.shape[1] * PAGE_ROWS
    paged = dict(page_table=page_table, fox_k=_key_minor(cache_fox_k), fox_v=_key_minor(cache_fox_v),
                 fox_logf=_key_minor(cache_fox_logf), mla_latent=cache_mla_latent,
                 mla_krope=_key_minor(cache_mla_krope))
    out_p = _run_trunk(x_prompt, mod_p, til_p, _rope_tables(jnp.arange(t, dtype=jnp.int32)), prm, zero_states, None)
    out_s = _run_trunk(x_sample, mod_s, til_s, _rope_tables(past_len + jnp.arange(dt, dtype=jnp.int32)), prm,
                       (state_conv_a, state_conv_d, state_ffn_conv), paged)
    return (out_p[0], out_s[0]) + tuple(out_p[1:]) + tuple(out_s[1:])
```
